```python
import math
import jax
import jax.numpy as jnp
from jax import lax
import numpy as np

D_MODEL = 4096
BATCH = 4
SEQ = 2048
DEPTH = 4
DEC_BATCH = 8
DEC_SEQ = 4
PAST_LEN = 8192
PAGE_SIZE = 128

A_HEAD_DIM = 128
A_HEADS = D_MODEL // 256
A_WIDTH = A_HEADS * A_HEAD_DIM
A_PATTERNS = ((128, 1), (512, 4), (2048, 16))
A_WINDOW = 2048
A_Q_BLOCK = 64
B_HEAD_DIM = 64
B_WIDTH = D_MODEL
B_HEADS = B_WIDTH // B_HEAD_DIM
B_GROUPS = 8
B_STATE = 128
B_CONV = 4
B_CHUNK = 128
B_CONV_DIM = B_WIDTH + 2 * B_GROUPS * B_STATE
C_QK_DIM = 128
C_V_DIM = 256
C_HEADS = D_MODEL // 256
C_QK_WIDTH = C_HEADS * 2 * C_QK_DIM
C_WIDTH = C_HEADS * C_V_DIM
C_Q_BLOCK = 128
N_EVEN = (DEPTH + 1) // 2
N_ODD = DEPTH // 2
ALPHA = (2 * DEPTH) ** 0.25
BETA = (8 * DEPTH) ** -0.25
EPS = 1e-5
EVEN_IN = 4 * A_WIDTH + B_WIDTH + B_CONV_DIM + B_HEADS
EVEN_SPLITS = (A_WIDTH, 2 * A_WIDTH, 3 * A_WIDTH, 4 * A_WIDTH, 4 * A_WIDTH + B_WIDTH, 4 * A_WIDTH + B_WIDTH + B_CONV_DIM)
EVEN_MIX = A_WIDTH + B_WIDTH
ODD_IN = 2 * C_QK_WIDTH + 2 * C_WIDTH
ODD_SPLITS = (C_QK_WIDTH, 2 * C_QK_WIDTH, 2 * C_QK_WIDTH + C_WIDTH)

kernel_name = 'hybrid_dilated_ssd_diffattn_decoder_step'


def alibi_slopes(n_heads):
    return jnp.exp2(-8.0 * jnp.arange(1, n_heads + 1, dtype=jnp.float32) / n_heads)


def layer_norm(x, g, b):
    xf = x.astype(jnp.float32)
    mu = jnp.mean(xf, -1, keepdims=True)
    var = jnp.mean(jnp.square(xf - mu), -1, keepdims=True)
    return ((xf - mu) * lax.rsqrt(var + EPS) * g.astype(jnp.float32) + b.astype(jnp.float32)).astype(x.dtype)


def rms_norm(x, g):
    xf = x.astype(jnp.float32)
    return xf * lax.rsqrt(jnp.mean(jnp.square(xf), -1, keepdims=True) + EPS) * g.astype(jnp.float32)


def modulate(x, c, w_mod, b_mod):
    m = jax.nn.silu(c) @ w_mod + b_mod
    shift, scale, gate = jnp.split(m[:, None, :], 3, axis=-1)
    return x * (1 + scale) + shift, 1 + gate


def dilated_pattern(q, k, v, q_idx, window, dilation, slopes):
    dist = jnp.arange(window // dilation + 1) * dilation
    idx = q_idx[:, None] - dist[None, :]
    valid = idx >= 0
    idx = jnp.maximum(idx, 0)
    kg = k[:, idx]
    vg = v[:, idx]
    s = jnp.einsum('bqhd,bqjhd->bhqj', q, kg).astype(jnp.float32) * (A_HEAD_DIM ** -0.5)
    s = s - slopes[:, None, None] * dist.astype(jnp.float32)
    s = jnp.where(valid, s, -jnp.inf)
    lse = jax.nn.logsumexp(s, axis=-1)
    p = jnp.exp(s - lse[..., None]).astype(v.dtype)
    return jnp.einsum('bhqj,bqjhd->bqhd', p, vg), lse


def dilated_mixture(q, k, v, q_idx, slopes):
    outs, lses = [], []
    for window, dilation in A_PATTERNS:
        o, l = dilated_pattern(q, k, v, q_idx, window, dilation, slopes)
        outs.append(o)
        lses.append(l)
    wts = jax.nn.softmax(jnp.stack(lses), axis=0)
    return jnp.einsum('gbhq,gbqhd->bqhd', wts.astype(q.dtype), jnp.stack(outs))


def dilated_prompt(q, k, v, slopes):
    t = q.shape[1]
    def block(start):
        qb = lax.dynamic_slice_in_dim(q, start, A_Q_BLOCK, axis=1)
        return dilated_mixture(qb, k, v, start + jnp.arange(A_Q_BLOCK), slopes)
    o = lax.map(block, jnp.arange(0, t, A_Q_BLOCK))
    return jnp.moveaxis(o, 0, 1).reshape(q.shape)


def causal_dwconv(xbc, buf, w, bias):
    t = xbc.shape[1]
    xp = jnp.concatenate([buf, xbc], axis=1)
    out = bias + xp[:, 0:t] * w[0]
    for j in range(1, B_CONV):
        out = out + xp[:, j:j + t] * w[j]
    return jax.nn.silu(out), xp[:, -(B_CONV - 1):]


def ssd_scan(x, dt, a_neg, bm, cm, h0, chunk):
    b, t, nh, p = x.shape
    g, n = bm.shape[2], bm.shape[3]
    r = nh // g
    nc = t // chunk
    xr = x.reshape(b, nc, chunk, g, r, p)
    dtr = dt.reshape(b, nc, chunk, g, r)
    br = bm.reshape(b, nc, chunk, g, n)
    cr = cm.reshape(b, nc, chunk, g, n)
    acum = jnp.cumsum(dtr * a_neg.reshape(g, r), axis=2)
    xdt = xr * dtr[..., None]
    causal = jnp.tril(jnp.ones((chunk, chunk), bool))[:, :, None, None]
    seg = acum[:, :, :, None] - acum[:, :, None, :]
    decay = jnp.exp(jnp.where(causal, seg, -jnp.inf))
    cb = jnp.einsum('bclgn,bcsgn->bclsg', cr, br)
    y_diag = jnp.einsum('bclsgr,bcsgrp->bclgrp', cb[..., None] * decay, xdt)
    w_end = xdt * jnp.exp(acum[:, :, -1:] - acum)[..., None]
    states = jnp.einsum('bclgn,bclgrp->bcgrpn', br, w_end)
    chunk_decay = jnp.exp(acum[:, :, -1])
    def step(h, inp):
        s_c, d_c = inp
        return h * d_c[..., None, None] + s_c, h
    h_fin, h_in = lax.scan(step, h0.reshape(b, g, r, p, n), (jnp.moveaxis(states, 1, 0), jnp.moveaxis(chunk_decay, 1, 0)))
    h_in = jnp.moveaxis(h_in, 0, 1)
    y_off = jnp.einsum('bclgn,bcgrpn->bclgrp', cr, h_in) * jnp.exp(acum)[..., None]
    return (y_diag + y_off).reshape(b, t, nh, p), h_fin.reshape(b, nh, p, n)


def even_mixer(h, w_in, w_out, conv_w, conv_b, dt_bias, a_log, d_skip, b_norm, slopes, past=None):
    f32 = jnp.float32
    bsz, t, _ = h.shape
    aq, ak, av, ag, z, xbc, dt_raw = jnp.split(h @ w_in, EVEN_SPLITS, axis=-1)
    heads = (bsz, t, A_HEADS, A_HEAD_DIM)
    aq, ak, av = aq.reshape(heads), ak.reshape(heads), av.reshape(heads)
    if past is None:
        o_a = dilated_prompt(aq, ak, av, slopes)
        k_all, v_all = ak, av
        conv_buf = jnp.zeros((bsz, B_CONV - 1, B_CONV_DIM), h.dtype)
        ssm0 = jnp.zeros((bsz, B_HEADS, B_HEAD_DIM, B_STATE), f32)
    else:
        win_k, win_v, conv_buf, ssm0 = past
        k_all = jnp.concatenate([win_k, ak], axis=1)
        v_all = jnp.concatenate([win_v, av], axis=1)
        o_a = dilated_mixture(aq, k_all, v_all, win_k.shape[1] + jnp.arange(t), slopes)
    keep = min(A_WINDOW, k_all.shape[1])
    new_wk, new_wv = k_all[:, -keep:], v_all[:, -keep:]
    y_a = o_a.reshape(bsz, t, A_WIDTH) * jax.nn.silu(ag)
    xbc, new_conv = causal_dwconv(xbc, conv_buf, conv_w, conv_b)
    xs, bm, cm = jnp.split(xbc.astype(f32), (B_WIDTH, B_WIDTH + B_GROUPS * B_STATE), axis=-1)
    xs = xs.reshape(bsz, t, B_HEADS, B_HEAD_DIM)
    bm = bm.reshape(bsz, t, B_GROUPS, B_STATE)
    cm = cm.reshape(bsz, t, B_GROUPS, B_STATE)
    dt = jax.nn.softplus(dt_raw.astype(f32) + dt_bias.astype(f32))
    a_neg = -jnp.exp(a_log.astype(f32))
    chunk = B_CHUNK if t % B_CHUNK == 0 else t
    y_b, new_ssm = ssd_scan(xs, dt, a_neg, bm, cm, ssm0.astype(f32), chunk)
    y_b = (y_b + d_skip.astype(f32)[:, None] * xs).reshape(bsz, t, B_WIDTH) * jax.nn.silu(z.astype(f32))
    y_b = rms_norm(y_b.reshape(bsz, t, B_GROUPS, B_WIDTH // B_GROUPS), b_norm.reshape(B_GROUPS, B_WIDTH // B_GROUPS))
    y_b = y_b.reshape(bsz, t, B_WIDTH).astype(h.dtype)
    y = jnp.concatenate([y_a, y_b], axis=-1) @ w_out
    return y, new_wk, new_wv, new_conv, new_ssm.astype(h.dtype)


def diff_attn(q, k, v, q_pos, k_pos, slopes, lam):
    s = jnp.einsum('bqhmd,bkhmd->bmhqk', q, k).astype(jnp.float32) * (C_QK_DIM ** -0.5)
    rel = q_pos[:, None] - k_pos[None, :]
    s = s - slopes[:, None, None] * rel.astype(jnp.float32)
    s = jnp.where(rel >= 0, s, -jnp.inf)
    p = jax.nn.softmax(s, axis=-1)
    attn = p[:, 0] - lam * p[:, 1]
    return jnp.einsum('bhqk,bkhd->bqhd', attn.astype(v.dtype), v)


def diff_prompt(q, k, v, slopes, lam):
    t = q.shape[1]
    k_pos = jnp.arange(t)
    def block(start):
        qb = lax.dynamic_slice_in_dim(q, start, C_Q_BLOCK, axis=1)
        return diff_attn(qb, k, v, start + jnp.arange(C_Q_BLOCK), k_pos, slopes, lam)
    o = lax.map(block, jnp.arange(0, t, C_Q_BLOCK))
    return jnp.moveaxis(o, 0, 1).reshape(q.shape[0], t, C_HEADS, C_V_DIM)


def odd_mixer(h, w_in, w_out, lam_q1, lam_k1, lam_q2, lam_k2, c_norm, lam_init, slopes, past=None):
    f32 = jnp.float32
    bsz, t, _ = h.shape
    q, k, v, g = jnp.split(h @ w_in, ODD_SPLITS, axis=-1)
    q = q.reshape(bsz, t, C_HEADS, 2, C_QK_DIM)
    k = k.reshape(bsz, t, C_HEADS, 2, C_QK_DIM)
    v = v.reshape(bsz, t, C_HEADS, C_V_DIM)
    lam = (jnp.exp(jnp.sum(lam_q1.astype(f32) * lam_k1.astype(f32)))
           - jnp.exp(jnp.sum(lam_q2.astype(f32) * lam_k2.astype(f32))) + lam_init)
    if past is None:
        o = diff_prompt(q, k, v, slopes, lam)
    else:
        k_past, v_past = past
        n_past = k_past.shape[1]
        o = diff_attn(q, jnp.concatenate([k_past, k], axis=1), jnp.concatenate([v_past, v], axis=1),
                      n_past + jnp.arange(t), jnp.arange(n_past + t), slopes, lam)
    o = (rms_norm(o, c_norm) * (1.0 - lam_init)).astype(h.dtype)
    y = (o.reshape(bsz, t, C_WIDTH) * jax.nn.silu(g)) @ w_out
    return y, k.reshape(bsz, t, C_HEADS, 2 * C_QK_DIM), v


def setup_inputs(seed: int = 0) -> dict:
    key = jax.random.key(seed)
    ks = iter(jax.random.split(key, 40))
    def nrm(shape, s=1.0):
        return jax.random.normal(next(ks), shape, jnp.float32) * s
    n_pages = PAST_LEN // PAGE_SIZE
    n_used = DEC_BATCH * n_pages
    n_pool = n_used + max(1, n_used // 4)
    win_buf = min(A_WINDOW, PAST_LEN)
    dt0 = jnp.exp(jax.random.uniform(next(ks), (N_EVEN, B_HEADS), jnp.float32, math.log(1e-3), math.log(1e-1)))
    a0 = jax.random.uniform(next(ks), (N_EVEN, B_HEADS), jnp.float32, 1.0, 16.0)
    page_table = jax.random.permutation(next(ks), n_pool)[:n_used].reshape(DEC_BATCH, n_pages).astype(jnp.int32)
    return {
        'x_prompt': nrm((BATCH, SEQ, D_MODEL)),
        'x_sample': nrm((DEC_BATCH, DEC_SEQ, D_MODEL)),
        'cache_win_k': nrm((N_EVEN, DEC_BATCH, win_buf, A_HEADS, A_HEAD_DIM)),
        'cache_win_v': nrm((N_EVEN, DEC_BATCH, win_buf, A_HEADS, A_HEAD_DIM)),
        'state_conv': nrm((N_EVEN, DEC_BATCH, B_CONV - 1, B_CONV_DIM)),
        'state_ssm': nrm((N_EVEN, DEC_BATCH, B_HEADS, B_HEAD_DIM, B_STATE), 0.5),
        'cache_diff_k': nrm((N_ODD, n_pool, PAGE_SIZE, C_HEADS, 2 * C_QK_DIM)),
        'cache_diff_v': nrm((N_ODD, n_pool, PAGE_SIZE, C_HEADS, C_V_DIM)),
        'page_table': page_table,
        'c_prompt': nrm((BATCH, D_MODEL)),
        'c_sample': nrm((DEC_BATCH, D_MODEL)),
        'ln_g': 1.0 + nrm((DEPTH, D_MODEL), 0.02),
        'ln_b': nrm((DEPTH, D_MODEL), 0.02),
        'w_mod': nrm((DEPTH, D_MODEL, 3 * D_MODEL), 0.1 * D_MODEL ** -0.5),
        'b_mod': nrm((DEPTH, 3 * D_MODEL), 0.01),
        'w_in_even': nrm((N_EVEN, D_MODEL, EVEN_IN), D_MODEL ** -0.5),
        'w_out_even': nrm((N_EVEN, EVEN_MIX, D_MODEL), BETA * EVEN_MIX ** -0.5),
        'conv_w': nrm((N_EVEN, B_CONV, B_CONV_DIM), B_CONV ** -0.5),
        'conv_b': nrm((N_EVEN, B_CONV_DIM), 0.01),
        'dt_bias': dt0 + jnp.log(-jnp.expm1(-dt0)),
        'a_log': jnp.log(a0),
        'd_skip': 1.0 + nrm((N_EVEN, B_HEADS), 0.1),
        'b_norm': 1.0 + nrm((N_EVEN, B_WIDTH), 0.02),
        'w_in_odd': nrm((N_ODD, D_MODEL, ODD_IN), D_MODEL ** -0.5),
        'w_out_odd': nrm((N_ODD, C_WIDTH, D_MODEL), BETA * C_WIDTH ** -0.5),
        'lam_q1': nrm((N_ODD, C_QK_DIM), 0.1),
        'lam_k1': nrm((N_ODD, C_QK_DIM), 0.1),
        'lam_q2': nrm((N_ODD, C_QK_DIM), 0.1),
        'lam_k2': nrm((N_ODD, C_QK_DIM), 0.1),
        'c_norm': 1.0 + nrm((N_ODD, C_V_DIM), 0.02),
    }


def reference(x_prompt, x_sample, cache_win_k, cache_win_v, state_conv, state_ssm, cache_diff_k, cache_diff_v,
              page_table, c_prompt, c_sample, ln_g, ln_b, w_mod, b_mod, w_in_even, w_out_even, conv_w, conv_b,
              dt_bias, a_log, d_skip, b_norm, w_in_odd, w_out_odd, lam_q1, lam_k1, lam_q2, lam_k2, c_norm):
    slopes_a = alibi_slopes(A_HEADS)
    slopes_c = alibi_slopes(C_HEADS)
    n_dec, n_pages = page_table.shape
    past_len = n_pages * cache_diff_k.shape[2]
    xp, xs = x_prompt, x_sample
    wk_p, wv_p, cv_p, ss_p, dk_p, dv_p = [], [], [], [], [], []
    wk_s, wv_s, cv_s, ss_s, dk_s, dv_s = [], [], [], [], [], []
    for l in range(DEPTH):
        hp, gp = modulate(xp, c_prompt, w_mod[l], b_mod[l])
        hs, gs = modulate(xs, c_sample, w_mod[l], b_mod[l])
        if l % 2 == 0:
            e = l // 2
            lw = (w_in_even[e], w_out_even[e], conv_w[e], conv_b[e], dt_bias[e], a_log[e], d_skip[e], b_norm[e], slopes_a)
            yp, a1, a2, a3, a4 = even_mixer(hp, *lw)
            ys, b1, b2, b3, b4 = even_mixer(hs, *lw, past=(cache_win_k[e], cache_win_v[e], state_conv[e], state_ssm[e]))
            wk_p.append(a1); wv_p.append(a2); cv_p.append(a3); ss_p.append(a4)
            wk_s.append(b1); wv_s.append(b2); cv_s.append(b3); ss_s.append(b4)
        else:
            o = l // 2
            lam_init = 0.8 - 0.6 * math.exp(-0.3 * l)
            lw = (w_in_odd[o], w_out_odd[o], lam_q1[o], lam_k1[o], lam_q2[o], lam_k2[o], c_norm[o], lam_init, slopes_c)
            k_past = cache_diff_k[o, page_table].reshape(n_dec, past_len, C_HEADS, 2, C_QK_DIM)
            v_past = cache_diff_v[o, page_table].reshape(n_dec, past_len, C_HEADS, C_V_DIM)
            yp, a1, a2 = odd_mixer(hp, *lw)
            ys, b1, b2 = odd_mixer(hs, *lw, past=(k_past, v_past))
            dk_p.append(a1); dv_p.append(a2)
            dk_s.append(b1); dv_s.append(b2)
        xp = layer_norm(ALPHA * xp + gp * yp, ln_g[l], ln_b[l])
        xs = layer_norm(ALPHA * xs + gs * ys, ln_g[l], ln_b[l])
    return (xp, xs,
            jnp.stack(wk_p), jnp.stack(wv_p), jnp.stack(cv_p), jnp.stack(ss_p), jnp.stack(dk_p), jnp.stack(dv_p),
            jnp.stack(wk_s), jnp.stack(wv_s), jnp.stack(cv_s), jnp.stack(ss_s), jnp.stack(dk_s), jnp.stack(dv_s))
```

```python
import functools
import math

import jax
import jax.numpy as jnp
from jax import lax
from jax.experimental import pallas as pl
from jax.experimental.pallas import tpu as pltpu

F32 = jnp.float32
BF16 = jnp.bfloat16
EPS = 1e-5
NEG = -1e30
A_PATTERNS = ((128, 1), (512, 4), (2048, 16))
A_HEAD_DIM = 128
B_HEAD_DIM = 64
B_GROUPS = 8
B_STATE = 128
B_CONV = 4
B_CHUNK = 128
C_QK_DIM = 128
C_V_DIM = 256
DEC_PAD = 8
LANES = 128
VMEM_BIG = 48 * 1024 * 1024
VMEM_MID = 32 * 1024 * 1024


def _params(sem, vmem=None):
    return pltpu.CompilerParams(dimension_semantics=sem, vmem_limit_bytes=vmem)


def _silu(x):
    return x * jax.nn.sigmoid(x)


def _div_pow2(x, n):
    assert n & (n - 1) == 0
    return x >> (n.bit_length() - 1)


def _mods_kernel(c_ref, w_ref, b_ref, o_ref, *, kc):
    c = c_ref[...]
    s = _silu(c).astype(BF16)
    d = c.shape[1]
    acc = jnp.zeros(o_ref.shape, F32)
    for k0 in range(0, d, kc):
        acc = acc + jnp.dot(s[:, k0:k0 + kc], w_ref[k0:k0 + kc, :].astype(BF16),
                            preferred_element_type=F32)
    o_ref[...] = acc + b_ref[...]


def _mods(c_rows, w_mod, b_mod, tn=512):
    nl, d, n = w_mod.shape
    r = c_rows.shape[0]
    return pl.pallas_call(
        functools.partial(_mods_kernel, kc=512),
        out_shape=jax.ShapeDtypeStruct((nl, r, n), F32),
        grid=(nl, n // tn),
        in_specs=[pl.BlockSpec((r, d), lambda l, j: (0, 0)),
                  pl.BlockSpec((None, d, tn), lambda l, j: (l, 0, j)),
                  pl.BlockSpec((None, 1, tn), lambda l, j: (l, 0, j))],
        out_specs=pl.BlockSpec((None, r, tn), lambda l, j: (l, 0, j)),
        compiler_params=_params(("arbitrary", "arbitrary"), VMEM_MID),
        name="mods",
    )(c_rows, w_mod, b_mod.reshape(nl, 1, n))


def _mm_kernel(*refs, k_sizes, cast_rows):
    nx = len(k_sizes)
    x_refs, w_ref, o_ref, wb_ref = refs[:nx], refs[nx], refs[nx + 1], refs[nx + 2]

    @pl.when(pl.program_id(1) == 0)
    def _():
        def body(r, carry):
            rows = pl.ds(pl.multiple_of(r * cast_rows, cast_rows), cast_rows)
            wb_ref[rows, :] = w_ref[rows, :].astype(BF16)
            return carry
        lax.fori_loop(0, wb_ref.shape[0] // cast_rows, body, 0)

    acc = None
    off = 0
    for x_ref, ks in zip(x_refs, k_sizes):
        part = jnp.dot(x_ref[...], wb_ref[off:off + ks, :], preferred_element_type=F32)
        acc = part if acc is None else acc + part
        off += ks
    o_ref[...] = acc


def _mm(x_parts, w, layer, tm, tn=512):
    m = x_parts[0].shape[0]
    k_sizes = tuple(x.shape[1] for x in x_parts)
    k = sum(k_sizes)
    n = w.shape[2]
    cast_rows = 256
    assert w.shape[1] == k and m % tm == 0 and k % cast_rows == 0
    in_specs = [pl.BlockSpec((tm, ks), lambda j, i: (i, 0)) for ks in k_sizes]
    in_specs.append(pl.BlockSpec((None, k, tn), lambda j, i: (layer, 0, j)))
    return pl.pallas_call(
        functools.partial(_mm_kernel, k_sizes=k_sizes, cast_rows=cast_rows),
        out_shape=jax.ShapeDtypeStruct((m, n), F32),
        grid=(pl.cdiv(n, tn), m // tm),
        in_specs=in_specs,
        out_specs=pl.BlockSpec((tm, tn), lambda j, i: (i, j)),
        scratch_shapes=[pltpu.VMEM((k, tn), BF16)],
        compiler_params=_params(("arbitrary", "arbitrary"), VMEM_BIG),
        name="proj_mm",
    )(*x_parts, w)


def _modulate_kernel(x_ref, sh_ref, sc_ref, h_ref):
    h_ref[...] = (x_ref[...] * (1.0 + sc_ref[...]) + sh_ref[...]).astype(h_ref.dtype)


def _post_kernel(x_ref, y_ref, gt_ref, g_ref, b_ref, *rest, alpha, with_next):
    if with_next:
        sh_ref, sc_ref, xo_ref, h_ref = rest
    else:
        (xo_ref,) = rest
    r = alpha * x_ref[...] + (1.0 + gt_ref[...]) * y_ref[...]
    mu = jnp.mean(r, axis=-1, keepdims=True)
    rc = r - mu
    var = jnp.mean(rc * rc, axis=-1, keepdims=True)
    xn = rc * lax.rsqrt(var + EPS) * g_ref[...] + b_ref[...]
    xo_ref[...] = xn
    if with_next:
        h_ref[...] = (xn * (1.0 + sc_ref[...]) + sh_ref[...]).astype(h_ref.dtype)


def _mod_spec(mods, layer, part, tm, rows_per_seq):
    d = mods.shape[3] // 3
    r = mods.shape[2]
    if r == 1:
        tiles = rows_per_seq // tm
        return pl.BlockSpec((None, None, 1, d), lambda i: (layer, i // tiles, 0, part))
    return pl.BlockSpec((None, None, r, d), lambda i: (layer, 0, i, part))


def _modulate(x, mods, layer, tm, rows_per_seq):
    m, d = x.shape
    row = pl.BlockSpec((tm, d), lambda i: (i, 0))
    return pl.pallas_call(
        _modulate_kernel,
        out_shape=jax.ShapeDtypeStruct((m, d), BF16),
        grid=(m // tm,),
        in_specs=[row, _mod_spec(mods, layer, 0, tm, rows_per_seq),
                  _mod_spec(mods, layer, 1, tm, rows_per_seq)],
        out_specs=row,
        compiler_params=_params(("arbitrary",), VMEM_MID),
        name="modulate",
    )(x, mods, mods)


def _post(x, y, mods, layer, ln_g, ln_b, alpha, tm, rows_per_seq, with_next):
    m, d = x.shape
    nl = ln_g.shape[0]
    row = pl.BlockSpec((tm, d), lambda i: (i, 0))
    vec = pl.BlockSpec((None, 1, d), lambda i: (layer, 0, 0))
    in_specs = [row, row, _mod_spec(mods, layer, 2, tm, rows_per_seq), vec, vec]
    args = [x, y, mods, ln_g.reshape(nl, 1, d), ln_b.reshape(nl, 1, d)]
    out_shape = [jax.ShapeDtypeStruct((m, d), F32)]
    out_specs = [row]
    if with_next:
        in_specs += [_mod_spec(mods, layer + 1, 0, tm, rows_per_seq),
                     _mod_spec(mods, layer + 1, 1, tm, rows_per_seq)]
        args += [mods, mods]
        out_shape.append(jax.ShapeDtypeStruct((m, d), BF16))
        out_specs.append(row)
    res = pl.pallas_call(
        functools.partial(_post_kernel, alpha=alpha, with_next=with_next),
        out_shape=out_shape,
        grid=(m // tm,),
        in_specs=in_specs,
        out_specs=out_specs,
        compiler_params=_params(("arbitrary",), VMEM_MID),
        name="post_ln",
    )(*args)
    return (res[0], res[1]) if with_next else (res[0], None)


def _pattern_count(dist):
    cnt = jnp.zeros(dist.shape, F32)
    for window, dil in A_PATTERNS:
        ok = (dist >= 0) & (dist <= window) & ((dist & (dil - 1)) == 0)
        cnt = cnt + jnp.where(ok, 1.0, 0.0)
    return cnt


def _softmax_step(s, weight, m_ref, l_ref):
    m_prev = m_ref[...]
    m_new = jnp.maximum(m_prev, jnp.max(s, axis=1, keepdims=True))
    p = jnp.exp(s - m_new) * weight
    alpha = jnp.exp(m_prev - m_new)
    l_ref[...] = alpha * l_ref[...] + jnp.sum(p, axis=1, keepdims=True)
    m_ref[...] = m_new
    return p, alpha


def _dil_prompt_kernel(slopes_ref, q_ref, k_ref, v_ref, g_ref, o_ref, m_ref, l_ref, acc_ref, *, tq, tk):
    h = pl.program_id(1)
    i = pl.program_id(2)
    slope = slopes_ref[h]
    scale = A_HEAD_DIM ** -0.5
    m_ref[...] = jnp.full(m_ref.shape, NEG, F32)
    l_ref[...] = jnp.zeros(l_ref.shape, F32)
    acc_ref[...] = jnp.zeros(acc_ref.shape, F32)
    q = q_ref[...].astype(BF16)
    rel = (lax.broadcasted_iota(jnp.int32, (tq, tk), 0) - lax.broadcasted_iota(jnp.int32, (tq, tk), 1))

    def body(j, carry):
        rows = pl.ds(pl.multiple_of(j * tk, tk), tk)
        k = k_ref[rows, :].astype(BF16)
        v = v_ref[rows, :].astype(BF16)
        s = lax.dot_general(q, k, (((1,), (1,)), ((), ())), preferred_element_type=F32) * scale
        dist = rel + (i * tq - j * tk)
        cnt = _pattern_count(dist)
        s = s - slope * dist.astype(F32)
        s = jnp.where(cnt > 0.0, s, NEG)
        p, alpha = _softmax_step(s, cnt, m_ref, l_ref)
        acc_ref[...] = alpha * acc_ref[...] + jnp.dot(p.astype(BF16), v, preferred_element_type=F32)
        return carry

    lax.fori_loop(0, (i * tq) // tk + tq // tk, body, 0)
    o = acc_ref[...] / l_ref[...]
    o_ref[...] = (o * _silu(g_ref[...])).astype(o_ref.dtype)


def _dil_prompt(proj, slopes, nb, t, heads, qc, kc, vc, gc, tq=256, tk=256):
    hd = A_HEAD_DIM
    nq = t // tq
    return pl.pallas_call(
        functools.partial(_dil_prompt_kernel, tq=tq, tk=tk),
        out_shape=jax.ShapeDtypeStruct((nb * t, heads * hd), BF16),
        grid=(nb, heads, nq),
        in_specs=[pl.BlockSpec(memory_space=pltpu.SMEM),
                  pl.BlockSpec((tq, hd), lambda b, h, i: (b * nq + i, qc + h)),
                  pl.BlockSpec((t, hd), lambda b, h, i: (b, kc + h)),
                  pl.BlockSpec((t, hd), lambda b, h, i: (b, vc + h)),
                  pl.BlockSpec((tq, hd), lambda b, h, i: (b * nq + i, gc + h))],
        out_specs=pl.BlockSpec((tq, hd), lambda b, h, i: (b * nq + i, h)),
        scratch_shapes=[pltpu.VMEM((tq, 1), F32), pltpu.VMEM((tq, 1), F32), pltpu.VMEM((tq, hd), F32)],
        compiler_params=_params(("arbitrary", "arbitrary", "arbitrary"), VMEM_MID),
        name="dilated_prompt",
    )(slopes, proj, proj, proj, proj)


def _dil_decode_kernel(slopes_ref, q_ref, g_ref, kc_ref, vc_ref, kn_ref, vn_ref, o_ref,
                       m_ref, l_ref, acc_ref, *, heads, past, kb):
    c = pl.program_id(1)
    nc = past * heads // kb
    scale = A_HEAD_DIM ** -0.5
    rq = q_ref.shape[0]

    @pl.when(c == 0)
    def _():
        m_ref[...] = jnp.full(m_ref.shape, NEG, F32)
        l_ref[...] = jnp.zeros(l_ref.shape, F32)
        acc_ref[...] = jnp.zeros(acc_ref.shape, F32)

    q = q_ref[...].astype(BF16)

    def attend(k, v, key_pos0):
        nk = k.shape[0]
        s = lax.dot_general(q, k.astype(BF16), (((1,), (1,)), ((), ())), preferred_element_type=F32) * scale
        row = lax.broadcasted_iota(jnp.int32, (rq, nk), 0)
        col = lax.broadcasted_iota(jnp.int32, (rq, nk), 1)
        qh = row & (heads - 1)
        kh = col & (heads - 1)
        dist = (past + _div_pow2(row, heads)) - (key_pos0 + _div_pow2(col, heads))
        cnt = jnp.where(qh == kh, _pattern_count(dist), 0.0)
        slope = jnp.zeros((rq, 1), F32)
        for hh in range(heads):
            slope = jnp.where(qh[:, :1] == hh, slopes_ref[hh], slope)
        s = s - slope * dist.astype(F32)
        s = jnp.where(cnt > 0.0, s, NEG)
        p, alpha = _softmax_step(s, cnt, m_ref, l_ref)
        acc_ref[...] = alpha * acc_ref[...] + jnp.dot(p.astype(BF16), v.astype(BF16),
                                                      preferred_element_type=F32)

    attend(kc_ref[...], vc_ref[...], c * (kb // heads))

    @pl.when(c == nc - 1)
    def _():
        attend(kn_ref[...], vn_ref[...], past)
        o = acc_ref[...] / l_ref[...]
        o_ref[...] = (o * _silu(g_ref[...])).astype(o_ref.dtype)


def _dil_decode(q, g, k_new, v_new, cache_k, cache_v, layer, slopes, heads, kb=2048):
    nb, rq, hd = q.shape
    past = cache_k.shape[2]
    ck = cache_k.reshape(cache_k.shape[0], nb, past * heads, hd)
    cv = cache_v.reshape(cache_v.shape[0], nb, past * heads, hd)
    small = pl.BlockSpec((None, rq, hd), lambda b, c: (b, 0, 0))
    cache = pl.BlockSpec((None, None, kb, hd), lambda b, c: (layer, b, c, 0))
    return pl.pallas_call(
        functools.partial(_dil_decode_kernel, heads=heads, past=past, kb=kb),
        out_shape=jax.ShapeDtypeStruct((nb, rq, hd), BF16),
        grid=(nb, past * heads // kb),
        in_specs=[pl.BlockSpec(memory_space=pltpu.SMEM), small, small, cache, cache, small, small],
        out_specs=small,
        scratch_shapes=[pltpu.VMEM((rq, 1), F32), pltpu.VMEM((rq, 1), F32), pltpu.VMEM((rq, hd), F32)],
        compiler_params=_params(("arbitrary", "arbitrary"), VMEM_MID),
        name="dilated_decode",
    )(slopes, q, g, ck, cv, k_new, v_new)


def _diff_finish(acc1, l1, acc2, l2, lam, lam_init, cn, g):
    o = acc1 / l1 - lam * (acc2 / l2)
    o = o * lax.rsqrt(jnp.mean(o * o, axis=-1, keepdims=True) + EPS) * cn * (1.0 - lam_init)
    return o * _silu(g)


def _diff_prompt_kernel(sc_ref, q_ref, k_ref, v_ref, g_ref, cn_ref, o_ref,
                        m1_ref, l1_ref, a1_ref, m2_ref, l2_ref, a2_ref, *, tq, tk, heads, lam_init):
    h = pl.program_id(1)
    i = pl.program_id(2)
    slope = sc_ref[h]
    lam = sc_ref[heads]
    scale = C_QK_DIM ** -0.5
    dk = C_QK_DIM
    for m_ref, l_ref, a_ref in ((m1_ref, l1_ref, a1_ref), (m2_ref, l2_ref, a2_ref)):
        m_ref[...] = jnp.full(m_ref.shape, NEG, F32)
        l_ref[...] = jnp.zeros(l_ref.shape, F32)
        a_ref[...] = jnp.zeros(a_ref.shape, F32)
    q1 = q_ref[:, :dk].astype(BF16)
    q2 = q_ref[:, dk:].astype(BF16)
    rel = (lax.broadcasted_iota(jnp.int32, (tq, tk), 0) - lax.broadcasted_iota(jnp.int32, (tq, tk), 1))

    def body(j, carry):
        rows = pl.ds(pl.multiple_of(j * tk, tk), tk)
        v = v_ref[rows, :].astype(BF16)
        dist = rel + (i * tq - j * tk)
        ok = dist >= 0
        w = jnp.where(ok, 1.0, 0.0)
        bias = slope * dist.astype(F32)
        for qm, c0, m_ref, l_ref, a_ref in ((q1, 0, m1_ref, l1_ref, a1_ref), (q2, dk, m2_ref, l2_ref, a2_ref)):
            k = k_ref[rows, c0:c0 + dk].astype(BF16)
            s = lax.dot_general(qm, k, (((1,), (1,)), ((), ())), preferred_element_type=F32) * scale
            s = jnp.where(ok, s - bias, NEG)
            p, alpha = _softmax_step(s, w, m_ref, l_ref)
            a_ref[...] = alpha * a_ref[...] + jnp.dot(p.astype(BF16), v, preferred_element_type=F32)
        return carry

    lax.fori_loop(0, (i * tq) // tk + tq // tk, body, 0)
    o_ref[...] = _diff_finish(a1_ref[...], l1_ref[...], a2_ref[...], l2_ref[...], lam, lam_init,
                              cn_ref[...], g_ref[...]).astype(o_ref.dtype)


def _diff_prompt(proj, scalars, c_norm, layer, nb, t, heads, qc, kc, vc, gc, lam_init, tq=256, tk=256):
    dv = C_V_DIM
    nq = t // tq
    cn = c_norm.reshape(c_norm.shape[0], 1, dv)
    return pl.pallas_call(
        functools.partial(_diff_prompt_kernel, tq=tq, tk=tk, heads=heads, lam_init=lam_init),
        out_shape=jax.ShapeDtypeStruct((nb * t, heads * dv), BF16),
        grid=(nb, heads, nq),
        in_specs=[pl.BlockSpec(memory_space=pltpu.SMEM),
                  pl.BlockSpec((tq, 2 * C_QK_DIM), lambda b, h, i: (b * nq + i, qc + h)),
                  pl.BlockSpec((t, 2 * C_QK_DIM), lambda b, h, i: (b, kc + h)),
                  pl.BlockSpec((t, dv), lambda b, h, i: (b, vc + h)),
                  pl.BlockSpec((tq, dv), lambda b, h, i: (b * nq + i, gc + h)),
                  pl.BlockSpec((None, 1, dv), lambda b, h, i: (layer, 0, 0))],
        out_specs=pl.BlockSpec((tq, dv), lambda b, h, i: (b * nq + i, h)),
        scratch_shapes=[pltpu.VMEM((tq, 1), F32), pltpu.VMEM((tq, 1), F32), pltpu.VMEM((tq, dv), F32),
                        pltpu.VMEM((tq, 1), F32), pltpu.VMEM((tq, 1), F32), pltpu.VMEM((tq, dv), F32)],
        compiler_params=_params(("arbitrary", "arbitrary", "arbitrary"), VMEM_MID),
        name="diff_prompt",
    )(scalars, proj, proj, proj, proj, cn)


def _diff_decode_kernel(pt_ref, sc_ref, q_ref, g_ref, cn_ref, kp_ref, vp_ref, kn_ref, vn_ref, o_ref,
                        m1_ref, l1_ref, a1_ref, m2_ref, l2_ref, a2_ref, *, heads, page, n_pages, lam_init):
    p_idx = pl.program_id(1)
    past = n_pages * page
    scale = C_QK_DIM ** -0.5
    dk = C_QK_DIM
    rq = q_ref.shape[0]
    lam = sc_ref[heads]

    @pl.when(p_idx == 0)
    def _():
        for m_ref, l_ref, a_ref in ((m1_ref, l1_ref, a1_ref), (m2_ref, l2_ref, a2_ref)):
            m_ref[...] = jnp.full(m_ref.shape, NEG, F32)
            l_ref[...] = jnp.zeros(l_ref.shape, F32)
            a_ref[...] = jnp.zeros(a_ref.shape, F32)

    q1 = q_ref[:, :dk].astype(BF16)
    q2 = q_ref[:, dk:].astype(BF16)

    def attend(k_ref, v_ref, key_pos0):
        nk = k_ref.shape[0]
        v = v_ref[...].astype(BF16)
        row = lax.broadcasted_iota(jnp.int32, (rq, nk), 0)
        col = lax.broadcasted_iota(jnp.int32, (rq, nk), 1)
        qh = row & (heads - 1)
        kh = col & (heads - 1)
        dist = (past + _div_pow2(row, heads)) - (key_pos0 + _div_pow2(col, heads))
        ok = (qh == kh) & (dist >= 0)
        w = jnp.where(ok, 1.0, 0.0)
        slope = jnp.zeros((rq, 1), F32)
        for hh in range(heads):
            slope = jnp.where(qh[:, :1] == hh, sc_ref[hh], slope)
        bias = slope * dist.astype(F32)
        for qm, c0, m_ref, l_ref, a_ref in ((q1, 0, m1_ref, l1_ref, a1_ref), (q2, dk, m2_ref, l2_ref, a2_ref)):
            k = k_ref[:, c0:c0 + dk].astype(BF16)
            s = lax.dot_general(qm, k, (((1,), (1,)), ((), ())), preferred_element_type=F32) * scale
            s = jnp.where(ok, s - bias, NEG)
            p, alpha = _softmax_step(s, w, m_ref, l_ref)
            a_ref[...] = alpha * a_ref[...] + jnp.dot(p.astype(BF16), v, preferred_element_type=F32)

    attend(kp_ref, vp_ref, p_idx * page)

    @pl.when(p_idx == n_pages - 1)
    def _():
        attend(kn_ref, vn_ref, past)
        o_ref[...] = _diff_finish(a1_ref[...], l1_ref[...], a2_ref[...], l2_ref[...], lam, lam_init,
                                  cn_ref[...], g_ref[...]).astype(o_ref.dtype)


def _diff_decode(q, g, k_new, v_new, cache_k, cache_v, page_table, layer, scalars, c_norm, heads, lam_init):
    nb, rq, _ = q.shape
    dv = C_V_DIM
    n_odd, n_pool, page = cache_k.shape[:3]
    n_pages = page_table.shape[1]
    ck = cache_k.reshape(n_odd, n_pool, page * heads, 2 * C_QK_DIM)
    cv = cache_v.reshape(n_odd, n_pool, page * heads, dv)
    cn = c_norm.reshape(c_norm.shape[0], 1, dv)
    small_k = pl.BlockSpec((None, rq, 2 * C_QK_DIM), lambda b, p, pt: (b, 0, 0))
    small_v = pl.BlockSpec((None, rq, dv), lambda b, p, pt: (b, 0, 0))
    grid_spec = pltpu.PrefetchScalarGridSpec(
        num_scalar_prefetch=1,
        grid=(nb, n_pages),
        in_specs=[pl.BlockSpec(memory_space=pltpu.SMEM), small_k, small_v,
                  pl.BlockSpec((None, 1, dv), lambda b, p, pt: (layer, 0, 0)),
                  pl.BlockSpec((None, None, page * heads, 2 * C_QK_DIM), lambda b, p, pt: (layer, pt[b, p], 0, 0)),
                  pl.BlockSpec((None, None, page * heads, dv), lambda b, p, pt: (layer, pt[b, p], 0, 0)),
                  small_k, small_v],
        out_specs=small_v,
        scratch_shapes=[pltpu.VMEM((rq, 1), F32), pltpu.VMEM((rq, 1), F32), pltpu.VMEM((rq, dv), F32),
                        pltpu.VMEM((rq, 1), F32), pltpu.VMEM((rq, 1), F32), pltpu.VMEM((rq, dv), F32)])
    return pl.pallas_call(
        functools.partial(_diff_decode_kernel, heads=heads, page=page, n_pages=n_pages, lam_init=lam_init),
        out_shape=jax.ShapeDtypeStruct((nb, rq, dv), BF16),
        grid_spec=grid_spec,
        compiler_params=_params(("arbitrary", "arbitrary"), VMEM_MID),
        name="diff_decode",
    )(page_table, scalars, q, g, cn, ck, cv, k_new, v_new)


def _expand_exact(x, onehot3):
    hi = x.astype(BF16)
    r1 = x - hi.astype(F32)
    mid = r1.astype(BF16)
    lo = (r1 - mid.astype(F32)).astype(BF16)
    return jnp.dot(jnp.concatenate([hi, mid, lo], axis=1), onehot3, preferred_element_type=F32)


def _ssd_kernel(xs_ref, bm_ref, cm_ref, z_ref, dt_ref, wx_ref, wb_ref, wc_ref, bx_ref, bb_ref, bc_ref,
                dtb_ref, alog_ref, dskip_ref, bnorm_ref, *rest, rows, t_valid, n_heads, has_past):
    if has_past:
        cx_ref, cbm_ref, ccm_ref, h0_ref, y_ref, hout_ref, px_ref, pb_ref, pc_ref, h_ref, at_ref = rest
    else:
        y_ref, hout_ref, px_ref, pb_ref, pc_ref, h_ref, at_ref = rest
    g = pl.program_id(1)
    c = pl.program_id(2)
    ln = B_CHUNK
    hpg = n_heads // B_GROUPS
    gw = hpg * B_HEAD_DIM
    pad = 8

    @pl.when(c == 0)
    def _():
        if has_past:
            px_ref[0:pad, :] = cx_ref[...]
            pb_ref[0:pad, :] = cbm_ref[...]
            pc_ref[0:pad, :] = ccm_ref[...]
            h_ref[...] = h0_ref[...]
        else:
            px_ref[0:pad, :] = jnp.zeros((pad, gw), F32)
            pb_ref[0:pad, :] = jnp.zeros((pad, B_STATE), F32)
            pc_ref[0:pad, :] = jnp.zeros((pad, B_STATE), F32)
            h_ref[...] = jnp.zeros(h_ref.shape, F32)
        if rows < ln:
            for p_ref in (px_ref, pb_ref, pc_ref):
                p_ref[pad + rows:pad + ln, :] = jnp.zeros((ln - rows, p_ref.shape[1]), F32)

    def conv(raw_ref, p_ref, w_ref, b_ref):
        p_ref[pad:pad + rows, :] = raw_ref[...]
        out = b_ref[...] + p_ref[pad - 3:pad - 3 + ln, :] * w_ref[0:1, :]
        for j in range(1, B_CONV):
            out = out + p_ref[pad - 3 + j:pad - 3 + j + ln, :] * w_ref[j:j + 1, :]
        if rows == ln:
            p_ref[0:pad, :] = p_ref[ln:ln + pad, :]
        return _silu(out)

    xs = conv(xs_ref, px_ref, wx_ref, bx_ref)
    bm = conv(bm_ref, pb_ref, wb_ref, bb_ref)
    cm = conv(cm_ref, pc_ref, wc_ref, bc_ref)

    lane = lax.broadcasted_iota(jnp.int32, (ln, LANES), 1)
    rowi = lax.broadcasted_iota(jnp.int32, (ln, LANES), 0)
    if rows < ln:
        dt_raw = jnp.concatenate([dt_ref[...], jnp.zeros((ln - rows, LANES), F32)], axis=0)
    else:
        dt_raw = dt_ref[...]
    live = (lane < n_heads) & (rowi < t_valid)
    dt = jnp.where(live, jax.nn.softplus(jnp.where(live, dt_raw, 0.0) + dtb_ref[...]), 0.0)
    a_step = dt * (-jnp.exp(alog_ref[...]))
    tri = jnp.where(lax.broadcasted_iota(jnp.int32, (ln, ln), 0) >= lax.broadcasted_iota(jnp.int32, (ln, ln), 1),
                    1.0, 0.0)
    acum = jnp.dot(tri, a_step, preferred_element_type=F32, precision=lax.Precision.HIGHEST)
    at_ref[...] = acum.T

    head_of_lane = g * hpg + _div_pow2(lax.broadcasted_iota(jnp.int32, (LANES, gw), 1), B_HEAD_DIM)
    oh_ch = jnp.where(lax.broadcasted_iota(jnp.int32, (LANES, gw), 0) == head_of_lane, 1.0, 0.0).astype(BF16)
    oh_ch3 = jnp.concatenate([oh_ch, oh_ch, oh_ch], axis=0)
    head_of_blk = g * hpg + _div_pow2(lax.broadcasted_iota(jnp.int32, (LANES, hpg * ln), 1), ln)
    oh_blk = jnp.where(lax.broadcasted_iota(jnp.int32, (LANES, hpg * ln), 0) == head_of_blk, 1.0, 0.0).astype(BF16)
    oh_blk3 = jnp.concatenate([oh_blk, oh_blk, oh_blk], axis=0)
    dt_e = _expand_exact(dt, oh_ch3)
    ac_e = _expand_exact(acum, oh_ch3)
    ac_col = _expand_exact(acum, oh_blk3)
    last_e = ac_e[ln - 1:ln, :]

    xdt = xs * dt_e
    xdt_b = xdt.astype(BF16)
    w_end = (xdt * jnp.exp(last_e - ac_e)).astype(BF16)
    bm_b = bm.astype(BF16)
    cm_b = cm.astype(BF16)
    cb = lax.dot_general(cm_b, bm_b, (((1,), (1,)), ((), ())), preferred_element_type=F32)
    h_prev = h_ref[...]
    y = jnp.dot(cm_b, h_prev.astype(BF16), preferred_element_type=F32) * jnp.exp(ac_e)

    causal = lax.broadcasted_iota(jnp.int32, (ln, ln), 0) >= lax.broadcasted_iota(jnp.int32, (ln, ln), 1)
    lane_pair = lax.broadcasted_iota(jnp.int32, (ln, 2 * B_HEAD_DIM), 1)
    y_parts = []
    for pr in range(hpg // 2):
        ms = []
        for r in (2 * pr, 2 * pr + 1):
            seg = ac_col[:, r * ln:(r + 1) * ln] - at_ref[pl.ds(g * hpg + r, 1), :]
            decay = jnp.exp(jnp.where(causal, seg, -jnp.inf))
            ms.append((cb * decay).astype(BF16))
        xp = xdt_b[:, pr * 2 * B_HEAD_DIM:(pr + 1) * 2 * B_HEAD_DIM]
        zero = jnp.zeros_like(xp)
        rhs = jnp.concatenate([jnp.where(lane_pair < B_HEAD_DIM, xp, zero),
                               jnp.where(lane_pair >= B_HEAD_DIM, xp, zero)], axis=0)
        y_parts.append(jnp.dot(jnp.concatenate(ms, axis=1), rhs, preferred_element_type=F32))
    y = y + jnp.concatenate(y_parts, axis=1)

    upd = jnp.dot(bm.T.astype(BF16), w_end, preferred_element_type=F32)
    h_new = h_prev * jnp.exp(last_e) + upd
    h_ref[...] = h_new
    hout_ref[...] = h_new

    y = (y + dskip_ref[...] * xs)
    if rows < ln:
        y = y[0:rows, :]
    y = y * _silu(z_ref[...])
    y = y * lax.rsqrt(jnp.mean(y * y, axis=-1, keepdims=True) + EPS) * bnorm_ref[...]
    y_ref[...] = y.astype(y_ref.dtype)


def _ssd(proj, nb, t_rows, t_valid, xc, bc, cc, zc, dtc, layer, conv_w, conv_b, dt_bias_p, a_log_p,
         d_skip_e, b_norm, past=None):
    ln = B_CHUNK
    rows = min(ln, t_rows)
    nc = t_rows // rows
    width = d_skip_e.shape[1]
    n_heads = width // B_HEAD_DIM
    gw = width // B_GROUPS
    st = B_STATE
    xoff = 0
    boff = width // st
    coff = boff + B_GROUPS
    ne = conv_w.shape[0]
    cbias = conv_b.reshape(ne, 1, conv_b.shape[1])
    row_blk = lambda b, g, c: b * nc + c
    in_specs = [
        pl.BlockSpec((rows, gw), lambda b, g, c: (row_blk(b, g, c), xc + g)),
        pl.BlockSpec((rows, st), lambda b, g, c: (row_blk(b, g, c), bc + g)),
        pl.BlockSpec((rows, st), lambda b, g, c: (row_blk(b, g, c), cc + g)),
        pl.BlockSpec((rows, gw), lambda b, g, c: (row_blk(b, g, c), zc + g)),
        pl.BlockSpec((rows, LANES), lambda b, g, c: (row_blk(b, g, c), dtc)),
        pl.BlockSpec((None, B_CONV, gw), lambda b, g, c: (layer, 0, xoff + g)),
        pl.BlockSpec((None, B_CONV, st), lambda b, g, c: (layer, 0, boff + g)),
        pl.BlockSpec((None, B_CONV, st), lambda b, g, c: (layer, 0, coff + g)),
        pl.BlockSpec((None, 1, gw), lambda b, g, c: (layer, 0, xoff + g)),
        pl.BlockSpec((None, 1, st), lambda b, g, c: (layer, 0, boff + g)),
        pl.BlockSpec((None, 1, st), lambda b, g, c: (layer, 0, coff + g)),
        pl.BlockSpec((None, 1, LANES), lambda b, g, c: (layer, 0, 0)),
        pl.BlockSpec((None, 1, LANES), lambda b, g, c: (layer, 0, 0)),
        pl.BlockSpec((None, 1, gw), lambda b, g, c: (layer, 0, g)),
        pl.BlockSpec((None, 1, gw), lambda b, g, c: (layer, 0, g)),
    ]
    args = [proj, proj, proj, proj, proj, conv_w, conv_w, conv_w, cbias, cbias, cbias,
            dt_bias_p, a_log_p, d_skip_e.reshape(ne, 1, width), b_norm.reshape(ne, 1, width)]
    if past is not None:
        conv_hist, h0t = past
        in_specs += [
            pl.BlockSpec((None, 8, gw), lambda b, g, c: (b, 0, xoff + g)),
            pl.BlockSpec((None, 8, st), lambda b, g, c: (b, 0, boff + g)),
            pl.BlockSpec((None, 8, st), lambda b, g, c: (b, 0, coff + g)),
            pl.BlockSpec((None, st, gw), lambda b, g, c: (b, 0, g)),
        ]
        args += [conv_hist, conv_hist, conv_hist, h0t]
    y, h_t = pl.pallas_call(
        functools.partial(_ssd_kernel, rows=rows, t_valid=t_valid, n_heads=n_heads, has_past=past is not None),
        out_shape=[jax.ShapeDtypeStruct((nb * t_rows, width), BF16),
                   jax.ShapeDtypeStruct((nb, st, width), F32)],
        grid=(nb, B_GROUPS, nc),
        in_specs=in_specs,
        out_specs=[pl.BlockSpec((rows, gw), lambda b, g, c: (row_blk(b, g, c), g)),
                   pl.BlockSpec((None, st, gw), lambda b, g, c: (b, 0, g))],
        scratch_shapes=[pltpu.VMEM((ln + 8, gw), F32), pltpu.VMEM((ln + 8, st), F32),
                        pltpu.VMEM((ln + 8, st), F32), pltpu.VMEM((st, gw), F32),
                        pltpu.VMEM((LANES, ln), F32)],
        compiler_params=_params(("arbitrary", "arbitrary", "arbitrary"), VMEM_MID),
        name="ssd",
    )(*args)
    return y, h_t


def _alibi_slopes(n_heads):
    return jnp.exp2(-8.0 * jnp.arange(1, n_heads + 1, dtype=F32) / n_heads)


def _pad_lanes(v):
    return jnp.pad(v, ((0, 0), (0, LANES - v.shape[1]))).reshape(v.shape[0], 1, LANES)


def kernel(x_prompt, x_sample, cache_win_k, cache_win_v, state_conv, state_ssm, cache_diff_k, cache_diff_v,
           page_table, c_prompt, c_sample, ln_g, ln_b, w_mod, b_mod, w_in_even, w_out_even, conv_w, conv_b,
           dt_bias, a_log, d_skip, b_norm, w_in_odd, w_out_odd, lam_q1, lam_k1, lam_q2, lam_k2, c_norm):
    nbp, seq, d = x_prompt.shape
    nbs, dec_seq, _ = x_sample.shape
    depth = w_mod.shape[0]
    alpha = (2 * depth) ** 0.25
    a_heads = cache_win_k.shape[3]
    a_width = a_heads * A_HEAD_DIM
    c_heads = cache_diff_k.shape[3]
    b_width = b_norm.shape[1]
    conv_dim = conv_w.shape[2]
    b_heads = dt_bias.shape[1]
    assert dec_seq <= DEC_PAD and seq % B_CHUNK == 0

    slopes_a = _alibi_slopes(a_heads)
    slopes_c = _alibi_slopes(c_heads)

    c_rows = jnp.concatenate([c_prompt, c_sample], axis=0)
    c_rows = jnp.pad(c_rows, ((0, (-c_rows.shape[0]) % 8), (0, 0)))
    mods = _mods(c_rows, w_mod, b_mod)
    mods_p = mods[:, :nbp].reshape(depth, nbp, 1, 3 * d)
    mods_s = jnp.repeat(mods[:, nbp:nbp + nbs], DEC_PAD, axis=1).reshape(depth, 1, nbs * DEC_PAD, 3 * d)

    xp = x_prompt.reshape(nbp * seq, d)
    xs = jnp.pad(x_sample, ((0, 0), (0, DEC_PAD - dec_seq), (0, 0))).reshape(nbs * DEC_PAD, d)
    ms_rows = nbs * DEC_PAD
    tm_row = 256

    hp = _modulate(xp, mods_p, 0, tm_row, seq)
    hs = _modulate(xs, mods_s, 0, ms_rows, ms_rows)

    dt_bias_p, a_log_p = _pad_lanes(dt_bias), _pad_lanes(a_log)
    d_skip_e = jnp.repeat(d_skip, B_HEAD_DIM, axis=1)
    conv_hist = jnp.pad(state_conv, ((0, 0), (0, 0), (8 - (B_CONV - 1), 0), (0, 0)))
    h0t = state_ssm.transpose(0, 1, 4, 2, 3).reshape(state_ssm.shape[0], nbs, B_STATE, b_width)

    def rows_th(a, width):
        return a.reshape(nbs, DEC_PAD * (a.shape[1] // width), width)

    outs = {k: [] for k in ("wk_p", "wv_p", "cv_p", "ss_p", "dk_p", "dv_p",
                            "wk_s", "wv_s", "cv_s", "ss_s", "dk_s", "dv_s")}
    for l in range(depth):
        last = l == depth - 1
        if l % 2 == 0:
            e = l // 2
            pp = _mm([hp], w_in_even, e, tm=1024)
            ps = _mm([hs], w_in_even, e, tm=ms_rows)
            aw = a_width // LANES
            z0 = 4 * a_width
            x0 = z0 + b_width
            ya_p = _dil_prompt(pp, slopes_a, nbp, seq, a_heads, 0, aw, 2 * aw, 3 * aw)
            gw = b_width // B_GROUPS
            ssd_cols = dict(xc=x0 // gw, bc=(x0 + b_width) // LANES, cc=(x0 + b_width) // LANES + B_GROUPS,
                            zc=z0 // gw, dtc=(x0 + conv_dim) // LANES)
            yb_p, ht_p = _ssd(pp, nbp, seq, seq, layer=e, conv_w=conv_w, conv_b=conv_b, dt_bias_p=dt_bias_p,
                              a_log_p=a_log_p, d_skip_e=d_skip_e, b_norm=b_norm, **ssd_cols)
            yp = _mm([ya_p, yb_p], w_out_even, e, tm=512)
            pp3 = pp.reshape(nbp, seq, -1)
            keep = min(A_PATTERNS[-1][0], seq)
            outs["wk_p"].append(pp3[:, seq - keep:, a_width:2 * a_width].reshape(nbp, keep, a_heads, A_HEAD_DIM))
            outs["wv_p"].append(pp3[:, seq - keep:, 2 * a_width:3 * a_width].reshape(nbp, keep, a_heads, A_HEAD_DIM))
            outs["cv_p"].append(pp3[:, seq - (B_CONV - 1):, x0:x0 + conv_dim])
            outs["ss_p"].append(ht_p.reshape(nbp, B_STATE, b_heads, B_HEAD_DIM).transpose(0, 2, 3, 1))
            q_s = rows_th(ps[:, 0:a_width], A_HEAD_DIM)
            k_s = rows_th(ps[:, a_width:2 * a_width], A_HEAD_DIM)
            v_s = rows_th(ps[:, 2 * a_width:3 * a_width], A_HEAD_DIM)
            g_s = rows_th(ps[:, 3 * a_width:4 * a_width], A_HEAD_DIM)
            ya_s = _dil_decode(q_s, g_s, k_s, v_s, cache_win_k, cache_win_v, e, slopes_a, a_heads)
            ya_s = ya_s.reshape(ms_rows, a_width)
            yb_s, ht_s = _ssd(ps, nbs, DEC_PAD, dec_seq, layer=e, conv_w=conv_w, conv_b=conv_b,
                              dt_bias_p=dt_bias_p, a_log_p=a_log_p, d_skip_e=d_skip_e, b_norm=b_norm,
                              past=(conv_hist[e], h0t[e]), **ssd_cols)
            ys = _mm([ya_s, yb_s], w_out_even, e, tm=ms_rows)
            ps3 = ps.reshape(nbs, DEC_PAD, -1)
            past = cache_win_k.shape[2]
            keep = min(A_PATTERNS[-1][0], past + dec_seq)
            k_new = ps3[:, :dec_seq, a_width:2 * a_width].reshape(nbs, dec_seq, a_heads, A_HEAD_DIM)
            v_new = ps3[:, :dec_seq, 2 * a_width:3 * a_width].reshape(nbs, dec_seq, a_heads, A_HEAD_DIM)
            outs["wk_s"].append(jnp.concatenate([cache_win_k[e], k_new], axis=1)[:, past + dec_seq - keep:])
            outs["wv_s"].append(jnp.concatenate([cache_win_v[e], v_new], axis=1)[:, past + dec_seq - keep:])
            xbc_all = jnp.concatenate([state_conv[e], ps3[:, :dec_seq, x0:x0 + conv_dim]], axis=1)
            outs["cv_s"].append(xbc_all[:, -(B_CONV - 1):])
            outs["ss_s"].append(ht_s.reshape(nbs, B_STATE, b_heads, B_HEAD_DIM).transpose(0, 2, 3, 1))
        else:
            o = l // 2
            lam_init = 0.8 - 0.6 * math.exp(-0.3 * l)
            lam = (jnp.exp(jnp.sum(lam_q1[o] * lam_k1[o])) - jnp.exp(jnp.sum(lam_q2[o] * lam_k2[o])) + lam_init)
            scalars = jnp.concatenate([slopes_c, lam.reshape(1)])
            pp = _mm([hp], w_in_odd, o, tm=1024)
            ps = _mm([hs], w_in_odd, o, tm=ms_rows)
            qkw = c_heads * 2 * C_QK_DIM
            vw = c_heads * C_V_DIM
            y_p = _diff_prompt(pp, scalars, c_norm, o, nbp, seq, c_heads, 0, c_heads, 2 * c_heads, 3 * c_heads,
                               lam_init)
            yp = _mm([y_p], w_out_odd, o, tm=1024)
            pp3 = pp.reshape(nbp, seq, -1)
            outs["dk_p"].append(pp3[:, :, qkw:2 * qkw].reshape(nbp, seq, c_heads, 2 * C_QK_DIM))
            outs["dv_p"].append(pp3[:, :, 2 * qkw:2 * qkw + vw].reshape(nbp, seq, c_heads, C_V_DIM))
            q_s = rows_th(ps[:, 0:qkw], 2 * C_QK_DIM)
            k_s = rows_th(ps[:, qkw:2 * qkw], 2 * C_QK_DIM)
            v_s = rows_th(ps[:, 2 * qkw:2 * qkw + vw], C_V_DIM)
            g_s = rows_th(ps[:, 2 * qkw + vw:], C_V_DIM)
            y_s = _diff_decode(q_s, g_s, k_s, v_s, cache_diff_k, cache_diff_v, page_table, o, scalars, c_norm,
                               c_heads, lam_init)
            ys = _mm([y_s.reshape(ms_rows, vw)], w_out_odd, o, tm=ms_rows)
            ps3 = ps.reshape(nbs, DEC_PAD, -1)
            outs["dk_s"].append(ps3[:, :dec_seq, qkw:2 * qkw].reshape(nbs, dec_seq, c_heads, 2 * C_QK_DIM))
            outs["dv_s"].append(ps3[:, :dec_seq, 2 * qkw:2 * qkw + vw].reshape(nbs, dec_seq, c_heads, C_V_DIM))
        xp, hp = _post(xp, yp, mods_p, l, ln_g, ln_b, alpha, tm_row, seq, not last)
        xs, hs = _post(xs, ys, mods_s, l, ln_g, ln_b, alpha, ms_rows, ms_rows, not last)

    y_prompt = xp.reshape(nbp, seq, d)
    y_sample = xs.reshape(nbs, DEC_PAD, d)[:, :dec_seq]
    st = lambda k: jnp.stack(outs[k])
    return (y_prompt, y_sample, st("wk_p"), st("wv_p"), st("cv_p"), st("ss_p"), st("dk_p"), st("dv_p"),
            st("wk_s"), st("wv_s"), st("cv_s"), st("ss_s"), st("dk_s"), st("dv_s"))
```

```python
import functools
import math

import jax
import jax.numpy as jnp
from jax import lax
from jax.experimental import pallas as pl
from jax.experimental.pallas import tpu as pltpu

F32 = jnp.float32
BF16 = jnp.bfloat16
EPS = 1e-5
NEG = -1e30
A_PATTERNS = ((128, 1), (512, 4), (2048, 16))
A_HEAD_DIM = 128
B_HEAD_DIM = 64
B_GROUPS = 8
B_STATE = 128
B_CONV = 4
B_CHUNK = 128
C_QK_DIM = 128
C_V_DIM = 256
DEC_PAD = 8
LANES = 128
VMEM_BIG = 48 * 1024 * 1024
VMEM_MID = 32 * 1024 * 1024


def _params(sem, vmem=None):
    return pltpu.CompilerParams(dimension_semantics=sem, vmem_limit_bytes=vmem)


def _silu(x):
    return x * jax.nn.sigmoid(x)


def _div_pow2(x, n):
    assert n & (n - 1) == 0
    return x >> (n.bit_length() - 1)


def _mods_kernel(c_ref, w_ref, b_ref, o_ref, *, kc):
    c = c_ref[...]
    s = _silu(c).astype(BF16)
    d = c.shape[1]
    acc = jnp.zeros(o_ref.shape, F32)
    for k0 in range(0, d, kc):
        acc = acc + jnp.dot(s[:, k0:k0 + kc], w_ref[k0:k0 + kc, :].astype(BF16),
                            preferred_element_type=F32)
    o_ref[...] = acc + b_ref[...]


def _mods(c_rows, w_mod, b_mod, tn=512):
    nl, d, n = w_mod.shape
    r = c_rows.shape[0]
    return pl.pallas_call(
        functools.partial(_mods_kernel, kc=512),
        out_shape=jax.ShapeDtypeStruct((nl, r, n), F32),
        grid=(nl, n // tn),
        in_specs=[pl.BlockSpec((r, d), lambda l, j: (0, 0)),
                  pl.BlockSpec((None, d, tn), lambda l, j: (l, 0, j)),
                  pl.BlockSpec((None, 1, tn), lambda l, j: (l, 0, j))],
        out_specs=pl.BlockSpec((None, r, tn), lambda l, j: (l, 0, j)),
        compiler_params=_params(("arbitrary", "arbitrary"), VMEM_MID),
        name="mods",
    )(c_rows, w_mod, b_mod.reshape(nl, 1, n))


def _mm_kernel(*refs, k_sizes, cast_rows):
    nx = len(k_sizes)
    x_refs, w_ref, o_ref, wb_ref = refs[:nx], refs[nx], refs[-2], refs[-1]

    @pl.when(pl.program_id(1) == 0)
    def _():
        def body(r, carry):
            rows = pl.ds(pl.multiple_of(r * cast_rows, cast_rows), cast_rows)
            wb_ref[rows, :] = w_ref[rows, :].astype(BF16)
            return carry
        lax.fori_loop(0, wb_ref.shape[0] // cast_rows, body, 0)

    acc = None
    off = 0
    for x_ref, ks in zip(x_refs, k_sizes):
        part = jnp.dot(x_ref[...], wb_ref[off:off + ks, :], preferred_element_type=F32)
        acc = part if acc is None else acc + part
        off += ks
    o_ref[...] = acc


def _mm(x_parts, w, layer, tm, col0=0, ncols=None, stack=None, tn=512):
    m = x_parts[0].shape[0]
    k_sizes = tuple(x.shape[1] for x in x_parts)
    k = sum(k_sizes)
    ncols = w.shape[2] - col0 if ncols is None else ncols
    cast_rows = 256
    assert w.shape[1] == k and m % tm == 0 and k % cast_rows == 0 and col0 % tn == 0
    cb0 = col0 // tn
    in_specs = [pl.BlockSpec((tm, ks), lambda j, i: (i, 0)) for ks in k_sizes]
    in_specs.append(pl.BlockSpec((None, k, tn), lambda j, i: (layer, 0, cb0 + j)))
    args = list(x_parts) + [w]
    aliases = {}
    if stack is None:
        out_shape = jax.ShapeDtypeStruct((m, ncols), F32)
        out_spec = pl.BlockSpec((tm, tn), lambda j, i: (i, j))
    else:
        buf, depth = stack
        assert ncols % tn == 0
        out_shape = jax.ShapeDtypeStruct((depth, m, ncols), F32)
        out_spec = pl.BlockSpec((None, tm, tn), lambda j, i: (layer, i, j))
        if buf is not None:
            in_specs.append(pl.BlockSpec(memory_space=pl.ANY))
            args.append(buf)
            aliases = {len(args) - 1: 0}
    return pl.pallas_call(
        functools.partial(_mm_kernel, k_sizes=k_sizes, cast_rows=cast_rows),
        out_shape=out_shape,
        grid=(pl.cdiv(ncols, tn), m // tm),
        in_specs=in_specs,
        out_specs=out_spec,
        scratch_shapes=[pltpu.VMEM((k, tn), BF16)],
        input_output_aliases=aliases,
        compiler_params=_params(("arbitrary", "arbitrary"), VMEM_BIG),
        name="proj_mm",
    )(*args)


def _modulate_kernel(x_ref, sh_ref, sc_ref, h_ref):
    h_ref[...] = (x_ref[...] * (1.0 + sc_ref[...]) + sh_ref[...]).astype(h_ref.dtype)


def _post_kernel(x_ref, y_ref, gt_ref, g_ref, b_ref, *rest, alpha, with_next):
    if with_next:
        sh_ref, sc_ref, xo_ref, h_ref = rest
    else:
        (xo_ref,) = rest
    r = alpha * x_ref[...] + (1.0 + gt_ref[...]) * y_ref[...]
    mu = jnp.mean(r, axis=-1, keepdims=True)
    rc = r - mu
    var = jnp.mean(rc * rc, axis=-1, keepdims=True)
    xn = rc * lax.rsqrt(var + EPS) * g_ref[...] + b_ref[...]
    xo_ref[...] = xn
    if with_next:
        h_ref[...] = (xn * (1.0 + sc_ref[...]) + sh_ref[...]).astype(h_ref.dtype)


def _mod_spec(mods, layer, part, tm, rows_per_seq):
    d = mods.shape[3] // 3
    r = mods.shape[2]
    if r == 1:
        tiles = rows_per_seq // tm
        return pl.BlockSpec((None, None, 1, d), lambda i: (layer, i // tiles, 0, part))
    return pl.BlockSpec((None, None, r, d), lambda i: (layer, 0, i, part))


def _modulate(x, mods, layer, tm, rows_per_seq):
    m, d = x.shape
    row = pl.BlockSpec((tm, d), lambda i: (i, 0))
    return pl.pallas_call(
        _modulate_kernel,
        out_shape=jax.ShapeDtypeStruct((m, d), BF16),
        grid=(m // tm,),
        in_specs=[row, _mod_spec(mods, layer, 0, tm, rows_per_seq),
                  _mod_spec(mods, layer, 1, tm, rows_per_seq)],
        out_specs=row,
        compiler_params=_params(("arbitrary",), VMEM_MID),
        name="modulate",
    )(x, mods, mods)


def _post(x, y, mods, layer, ln_g, ln_b, alpha, tm, rows_per_seq, with_next):
    m, d = x.shape
    nl = ln_g.shape[0]
    row = pl.BlockSpec((tm, d), lambda i: (i, 0))
    vec = pl.BlockSpec((None, 1, d), lambda i: (layer, 0, 0))
    in_specs = [row, row, _mod_spec(mods, layer, 2, tm, rows_per_seq), vec, vec]
    args = [x, y, mods, ln_g.reshape(nl, 1, d), ln_b.reshape(nl, 1, d)]
    out_shape = [jax.ShapeDtypeStruct((m, d), F32)]
    out_specs = [row]
    if with_next:
        in_specs += [_mod_spec(mods, layer + 1, 0, tm, rows_per_seq),
                     _mod_spec(mods, layer + 1, 1, tm, rows_per_seq)]
        args += [mods, mods]
        out_shape.append(jax.ShapeDtypeStruct((m, d), BF16))
        out_specs.append(row)
    res = pl.pallas_call(
        functools.partial(_post_kernel, alpha=alpha, with_next=with_next),
        out_shape=out_shape,
        grid=(m // tm,),
        in_specs=in_specs,
        out_specs=out_specs,
        compiler_params=_params(("arbitrary",), VMEM_MID),
        name="post_ln",
    )(*args)
    return (res[0], res[1]) if with_next else (res[0], None)


def _pattern_count(dist):
    cnt = jnp.zeros(dist.shape, F32)
    for window, dil in A_PATTERNS:
        ok = (dist >= 0) & (dist <= window) & ((dist & (dil - 1)) == 0)
        cnt = cnt + jnp.where(ok, 1.0, 0.0)
    return cnt


def _softmax_step(s, weight, m_ref, l_ref):
    m_prev = m_ref[...]
    m_new = jnp.maximum(m_prev, jnp.max(s, axis=1, keepdims=True))
    p = jnp.exp(s - m_new) * weight
    alpha = jnp.exp(m_prev - m_new)
    l_ref[...] = alpha * l_ref[...] + jnp.sum(p, axis=1, keepdims=True)
    m_ref[...] = m_new
    return p, alpha


M_FLOOR = -1e20
LOG2E = math.log2(math.e)


def _flash_t_tile(k_maps, v_t, q_maps, bias_t, cj, scale, stats):
    ps, alphas = [], []
    for k, q, (m_ref, l_ref, _) in zip(k_maps, q_maps, stats):
        s = lax.dot_general(k, q, (((1,), (1,)), ((), ())), preferred_element_type=F32) * (scale * LOG2E) + bias_t
        m_prev = m_ref[...]
        m_new = jnp.maximum(m_prev, jnp.max(s, axis=0, keepdims=True) - cj)
        p = jnp.exp2(s - (m_new + cj))
        alpha = jnp.exp2(m_prev - m_new)
        l_ref[...] = alpha * l_ref[...] + jnp.sum(p, axis=0, keepdims=True)
        m_ref[...] = m_new
        ps.append(p.astype(BF16))
        alphas.append(alpha)
    pv = jnp.dot(v_t, ps[0] if len(ps) == 1 else jnp.concatenate(ps, axis=1), preferred_element_type=F32)
    tq = ps[0].shape[1]
    for n, (alpha, (_, _, acc_ref)) in enumerate(zip(alphas, stats)):
        acc_ref[...] = alpha * acc_ref[...] + pv[:, n * tq:(n + 1) * tq]


def _flash_t_reset(stats):
    for m_ref, l_ref, acc_ref in stats:
        m_ref[...] = jnp.full(m_ref.shape, M_FLOOR, F32)
        l_ref[...] = jnp.zeros(l_ref.shape, F32)
        acc_ref[...] = jnp.zeros(acc_ref.shape, F32)


def _stage_kv(k_ref, v_ref, kb_ref, vt_ref):
    t = k_ref.shape[0]
    for r0 in range(0, t, 256):
        kb_ref[r0:r0 + 256, :] = k_ref[r0:r0 + 256, :].astype(BF16)
    for r0 in range(0, t, LANES):
        vt_ref[:, r0:r0 + LANES] = v_ref[r0:r0 + LANES, :].T.astype(BF16)


def _dil_prompt_kernel(slopes_ref, q_ref, k_ref, v_ref, g_ref, logc_ref, o_ref,
                       kb_ref, vt_ref, bias_ref, m_ref, l_ref, acc_ref, *, tile, window):
    h = pl.program_id(1)
    i = pl.program_id(2)
    slope = slopes_ref[h]
    n_off = bias_ref.shape[0]

    @pl.when(i == 0)
    def _():
        _stage_kv(k_ref, v_ref, kb_ref, vt_ref)
        rel_t = (lax.broadcasted_iota(jnp.int32, (tile, tile), 1)
                 - lax.broadcasted_iota(jnp.int32, (tile, tile), 0)).astype(F32)
        for d in range(n_off):
            bias_ref[d] = (logc_ref[d] - slope * rel_t) * LOG2E

    stats = ((m_ref, l_ref, acc_ref),)
    _flash_t_reset(stats)
    q = q_ref[...].astype(BF16)

    def body(j, carry):
        rows = pl.ds(pl.multiple_of(j * tile, tile), tile)
        cj = (slope * LOG2E) * ((i - j) * tile).astype(F32)
        _flash_t_tile((kb_ref[rows, :],), vt_ref[:, rows], (q,), bias_ref[i - j], cj,
                      A_HEAD_DIM ** -0.5, stats)
        return carry

    lax.fori_loop(jnp.maximum(i - (n_off - 1), 0), i + 1, body, 0)
    o = (acc_ref[...] * (1.0 / l_ref[...])).T
    o_ref[...] = (o * _silu(g_ref[...])).astype(o_ref.dtype)


def _log_count_tiles(tile, n_off):
    import numpy as np
    dist = (np.arange(n_off)[:, None, None] * tile + np.arange(tile)[None, None, :] - np.arange(tile)[None, :, None])
    cnt = np.zeros(dist.shape, np.float64)
    for window, dil in A_PATTERNS:
        cnt += (dist >= 0) & (dist <= window) & (dist % dil == 0)
    return jnp.asarray(np.where(cnt > 0, np.log(np.maximum(cnt, 1.0)), NEG), F32)


def _dil_prompt(q, k_stack, v_stack, g, layer, slopes, nb, t, heads, gc, tile=512):
    hd = A_HEAD_DIM
    nq = t // tile
    window = A_PATTERNS[-1][0]
    n_off = min(nq, window // tile + 2)
    logc = _log_count_tiles(tile, n_off)
    return pl.pallas_call(
        functools.partial(_dil_prompt_kernel, tile=tile, window=window),
        out_shape=jax.ShapeDtypeStruct((nb * t, heads * hd), BF16),
        grid=(nb, heads, nq),
        in_specs=[pl.BlockSpec(memory_space=pltpu.SMEM),
                  pl.BlockSpec((tile, hd), lambda b, h, i: (b * nq + i, h)),
                  pl.BlockSpec((None, t, hd), lambda b, h, i: (layer, b, h)),
                  pl.BlockSpec((None, t, hd), lambda b, h, i: (layer, b, h)),
                  pl.BlockSpec((tile, hd), lambda b, h, i: (b * nq + i, gc + h)),
                  pl.BlockSpec((n_off, tile, tile), lambda b, h, i: (0, 0, 0))],
        out_specs=pl.BlockSpec((tile, hd), lambda b, h, i: (b * nq + i, h)),
        scratch_shapes=[pltpu.VMEM((t, hd), BF16), pltpu.VMEM((hd, t), BF16),
                        pltpu.VMEM((n_off, tile, tile), F32),
                        pltpu.VMEM((1, tile), F32), pltpu.VMEM((1, tile), F32), pltpu.VMEM((hd, tile), F32)],
        compiler_params=_params(("arbitrary", "arbitrary", "arbitrary"), VMEM_BIG),
        name="dilated_prompt",
    )(slopes, q, k_stack, v_stack, g, logc)


def _dil_decode_kernel(slopes_ref, q_ref, g_ref, kc_ref, vc_ref, kn_ref, vn_ref, o_ref,
                       m_ref, l_ref, acc_ref, *, heads, past, kb):
    c = pl.program_id(1)
    nc = past * heads // kb
    scale = A_HEAD_DIM ** -0.5
    rq = q_ref.shape[0]

    @pl.when(c == 0)
    def _():
        m_ref[...] = jnp.full(m_ref.shape, NEG, F32)
        l_ref[...] = jnp.zeros(l_ref.shape, F32)
        acc_ref[...] = jnp.zeros(acc_ref.shape, F32)

    q = q_ref[...].astype(BF16)

    def attend(k, v, key_pos0):
        nk = k.shape[0]
        s = lax.dot_general(q, k.astype(BF16), (((1,), (1,)), ((), ())), preferred_element_type=F32) * scale
        row = lax.broadcasted_iota(jnp.int32, (rq, nk), 0)
        col = lax.broadcasted_iota(jnp.int32, (rq, nk), 1)
        qh = row & (heads - 1)
        kh = col & (heads - 1)
        dist = (past + _div_pow2(row, heads)) - (key_pos0 + _div_pow2(col, heads))
        cnt = jnp.where(qh == kh, _pattern_count(dist), 0.0)
        slope = jnp.zeros((rq, 1), F32)
        for hh in range(heads):
            slope = jnp.where(qh[:, :1] == hh, slopes_ref[hh], slope)
        s = s - slope * dist.astype(F32)
        s = jnp.where(cnt > 0.0, s, NEG)
        p, alpha = _softmax_step(s, cnt, m_ref, l_ref)
        acc_ref[...] = alpha * acc_ref[...] + jnp.dot(p.astype(BF16), v.astype(BF16),
                                                      preferred_element_type=F32)

    attend(kc_ref[...], vc_ref[...], c * (kb // heads))

    @pl.when(c == nc - 1)
    def _():
        attend(kn_ref[...], vn_ref[...], past)
        o = acc_ref[...] / l_ref[...]
        o_ref[...] = (o * _silu(g_ref[...])).astype(o_ref.dtype)


def _dil_decode(q, g, k_new, v_new, cache_k, cache_v, layer, slopes, heads, kb=2048):
    nb, rq, hd = q.shape
    past = cache_k.shape[2]
    ck = cache_k.reshape(cache_k.shape[0], nb, past * heads, hd)
    cv = cache_v.reshape(cache_v.shape[0], nb, past * heads, hd)
    small = pl.BlockSpec((None, rq, hd), lambda b, c: (b, 0, 0))
    cache = pl.BlockSpec((None, None, kb, hd), lambda b, c: (layer, b, c, 0))
    return pl.pallas_call(
        functools.partial(_dil_decode_kernel, heads=heads, past=past, kb=kb),
        out_shape=jax.ShapeDtypeStruct((nb, rq, hd), BF16),
        grid=(nb, past * heads // kb),
        in_specs=[pl.BlockSpec(memory_space=pltpu.SMEM), small, small, cache, cache, small, small],
        out_specs=small,
        scratch_shapes=[pltpu.VMEM((rq, 1), F32), pltpu.VMEM((rq, 1), F32), pltpu.VMEM((rq, hd), F32)],
        compiler_params=_params(("arbitrary", "arbitrary"), VMEM_MID),
        name="dilated_decode",
    )(slopes, q, g, ck, cv, k_new, v_new)


def _diff_finish(acc1, l1, acc2, l2, lam, lam_init, cn, g):
    o = acc1 / l1 - lam * (acc2 / l2)
    o = o * lax.rsqrt(jnp.mean(o * o, axis=-1, keepdims=True) + EPS) * cn * (1.0 - lam_init)
    return o * _silu(g)


def _diff_prompt_kernel(sc_ref, q_ref, k_ref, v_ref, g_ref, cn_ref, o_ref, kb_ref, vt_ref, bias_ref,
                        m1_ref, l1_ref, a1_ref, m2_ref, l2_ref, a2_ref, *, tile, heads, lam_init):
    h = pl.program_id(1)
    i = pl.program_id(2)
    slope = sc_ref[h]
    lam = sc_ref[heads]
    dk = C_QK_DIM

    @pl.when(i == 0)
    def _():
        _stage_kv(k_ref, v_ref, kb_ref, vt_ref)
        rel_t = (lax.broadcasted_iota(jnp.int32, (tile, tile), 1)
                 - lax.broadcasted_iota(jnp.int32, (tile, tile), 0))
        below = (-slope * LOG2E) * rel_t.astype(F32)
        bias_ref[0] = jnp.where(rel_t >= 0, below, NEG)
        bias_ref[1] = below

    stats = ((m1_ref, l1_ref, a1_ref), (m2_ref, l2_ref, a2_ref))
    _flash_t_reset(stats)
    qs = (q_ref[:, :dk].astype(BF16), q_ref[:, dk:].astype(BF16))

    def tile_step(j, bias_t):
        rows = pl.ds(pl.multiple_of(j * tile, tile), tile)
        cj = (slope * LOG2E) * ((i - j) * tile).astype(F32)
        _flash_t_tile((kb_ref[rows, :dk], kb_ref[rows, dk:]), vt_ref[:, rows], qs, bias_t, cj,
                      C_QK_DIM ** -0.5, stats)

    def body(j, carry):
        tile_step(j, bias_ref[1])
        return carry

    lax.fori_loop(0, i, body, 0)
    tile_step(i, bias_ref[0])
    o_t = a1_ref[...] * (1.0 / l1_ref[...]) - lam * (a2_ref[...] * (1.0 / l2_ref[...]))
    o_t = o_t * lax.rsqrt(jnp.mean(o_t * o_t, axis=0, keepdims=True) + EPS)
    cn = cn_ref[...]
    o_t = o_t * jnp.concatenate([cn] * (tile // LANES), axis=1) * (1.0 - lam_init)
    o_ref[...] = (o_t.T * _silu(g_ref[...])).astype(o_ref.dtype)


def _diff_prompt(q, k_stack, v_stack, g, layer, scalars, c_norm, nb, t, heads, lam_init, tile=512):
    dv = C_V_DIM
    nq = t // tile
    cn = jnp.broadcast_to(c_norm[:, :, None], (c_norm.shape[0], dv, LANES))
    return pl.pallas_call(
        functools.partial(_diff_prompt_kernel, tile=tile, heads=heads, lam_init=lam_init),
        out_shape=jax.ShapeDtypeStruct((nb * t, heads * dv), BF16),
        grid=(nb, heads, nq),
        in_specs=[pl.BlockSpec(memory_space=pltpu.SMEM),
                  pl.BlockSpec((tile, 2 * C_QK_DIM), lambda b, h, i: (b * nq + i, h)),
                  pl.BlockSpec((None, t, 2 * C_QK_DIM), lambda b, h, i: (layer, b, h)),
                  pl.BlockSpec((None, t, dv), lambda b, h, i: (layer, b, h)),
                  pl.BlockSpec((tile, dv), lambda b, h, i: (b * nq + i, h)),
                  pl.BlockSpec((None, dv, LANES), lambda b, h, i: (layer, 0, 0))],
        out_specs=pl.BlockSpec((tile, dv), lambda b, h, i: (b * nq + i, h)),
        scratch_shapes=[pltpu.VMEM((t, 2 * C_QK_DIM), BF16), pltpu.VMEM((dv, t), BF16),
                        pltpu.VMEM((2, tile, tile), F32),
                        pltpu.VMEM((1, tile), F32), pltpu.VMEM((1, tile), F32), pltpu.VMEM((dv, tile), F32),
                        pltpu.VMEM((1, tile), F32), pltpu.VMEM((1, tile), F32), pltpu.VMEM((dv, tile), F32)],
        compiler_params=_params(("arbitrary", "arbitrary", "arbitrary"), VMEM_BIG),
        name="diff_prompt",
    )(scalars, q, k_stack, v_stack, g, cn)


def _diff_decode_kernel(pt_ref, sc_ref, q_ref, g_ref, cn_ref, kp_ref, vp_ref, kn_ref, vn_ref, o_ref,
                        bias_ref, slope_ref, m1_ref, l1_ref, a1_ref, m2_ref, l2_ref, a2_ref,
                        *, heads, page, n_pages, lam_init):
    p_idx = pl.program_id(1)
    past = n_pages * page
    scale = C_QK_DIM ** -0.5
    dk = C_QK_DIM
    rq = q_ref.shape[0]
    lam = sc_ref[heads]
    stats = ((m1_ref, l1_ref, a1_ref), (m2_ref, l2_ref, a2_ref))

    def head_and_offset(nk):
        row = lax.broadcasted_iota(jnp.int32, (rq, nk), 0)
        col = lax.broadcasted_iota(jnp.int32, (rq, nk), 1)
        same_head = (row & (heads - 1)) == (col & (heads - 1))
        return same_head, _div_pow2(row, heads) - _div_pow2(col, heads)

    @pl.when(p_idx == 0)
    def _():
        for m_ref, l_ref, a_ref in stats:
            m_ref[...] = jnp.full(m_ref.shape, M_FLOOR, F32)
            l_ref[...] = jnp.zeros(l_ref.shape, F32)
            a_ref[...] = jnp.zeros(a_ref.shape, F32)
        qh = lax.broadcasted_iota(jnp.int32, (rq, 1), 0) & (heads - 1)
        slope = jnp.zeros((rq, 1), F32)
        for hh in range(heads):
            slope = jnp.where(qh == hh, sc_ref[hh], slope)
        slope_ref[...] = slope
        same_head, off = head_and_offset(page * heads)
        bias_ref[...] = jnp.where(same_head, -slope * off.astype(F32), NEG)

    q1 = q_ref[:, :dk].astype(BF16)
    q2 = q_ref[:, dk:].astype(BF16)

    def attend(k_ref, v_ref, bias, shift):
        v = v_ref[...].astype(BF16)
        for qm, c0, (m_ref, l_ref, a_ref) in ((q1, 0, stats[0]), (q2, dk, stats[1])):
            k = k_ref[:, c0:c0 + dk].astype(BF16)
            s = lax.dot_general(qm, k, (((1,), (1,)), ((), ())), preferred_element_type=F32) * scale + bias
            m_prev = m_ref[...]
            m_new = jnp.maximum(m_prev, jnp.max(s, axis=1, keepdims=True) - shift)
            p = jnp.exp(s - (m_new + shift))
            alpha = jnp.exp(m_prev - m_new)
            l_ref[...] = alpha * l_ref[...] + jnp.sum(p, axis=1, keepdims=True)
            m_ref[...] = m_new
            a_ref[...] = alpha * a_ref[...] + jnp.dot(p.astype(BF16), v, preferred_element_type=F32)

    attend(kp_ref, vp_ref, bias_ref[...], slope_ref[...] * (past - p_idx * page).astype(F32))

    @pl.when(p_idx == n_pages - 1)
    def _():
        same_head, off = head_and_offset(kn_ref.shape[0])
        bias_new = jnp.where(same_head & (off >= 0), -slope_ref[...] * off.astype(F32), NEG)
        attend(kn_ref, vn_ref, bias_new, jnp.zeros((rq, 1), F32))
        o_ref[...] = _diff_finish(a1_ref[...], l1_ref[...], a2_ref[...], l2_ref[...], lam, lam_init,
                                  cn_ref[...], g_ref[...]).astype(o_ref.dtype)


def _diff_decode(q, g, k_new, v_new, cache_k, cache_v, page_table, layer, scalars, c_norm, heads, lam_init):
    nb, rq, _ = q.shape
    dv = C_V_DIM
    n_odd, n_pool, page = cache_k.shape[:3]
    n_pages = page_table.shape[1]
    ck = cache_k.reshape(n_odd, n_pool, page * heads, 2 * C_QK_DIM)
    cv = cache_v.reshape(n_odd, n_pool, page * heads, dv)
    cn = c_norm.reshape(c_norm.shape[0], 1, dv)
    small_k = pl.BlockSpec((None, rq, 2 * C_QK_DIM), lambda b, p, pt: (b, 0, 0))
    small_v = pl.BlockSpec((None, rq, dv), lambda b, p, pt: (b, 0, 0))
    grid_spec = pltpu.PrefetchScalarGridSpec(
        num_scalar_prefetch=1,
        grid=(nb, n_pages),
        in_specs=[pl.BlockSpec(memory_space=pltpu.SMEM), small_k, small_v,
                  pl.BlockSpec((None, 1, dv), lambda b, p, pt: (layer, 0, 0)),
                  pl.BlockSpec((None, None, page * heads, 2 * C_QK_DIM), lambda b, p, pt: (layer, pt[b, p], 0, 0)),
                  pl.BlockSpec((None, None, page * heads, dv), lambda b, p, pt: (layer, pt[b, p], 0, 0)),
                  small_k, small_v],
        out_specs=small_v,
        scratch_shapes=[pltpu.VMEM((rq, page * heads), F32), pltpu.VMEM((rq, 1), F32),
                        pltpu.VMEM((rq, 1), F32), pltpu.VMEM((rq, 1), F32), pltpu.VMEM((rq, dv), F32),
                        pltpu.VMEM((rq, 1), F32), pltpu.VMEM((rq, 1), F32), pltpu.VMEM((rq, dv), F32)])
    return pl.pallas_call(
        functools.partial(_diff_decode_kernel, heads=heads, page=page, n_pages=n_pages, lam_init=lam_init),
        out_shape=jax.ShapeDtypeStruct((nb, rq, dv), BF16),
        grid_spec=grid_spec,
        compiler_params=_params(("arbitrary", "arbitrary"), VMEM_MID),
        name="diff_decode",
    )(page_table, scalars, q, g, cn, ck, cv, k_new, v_new)


def _expand_exact(x, onehot3):
    hi = x.astype(BF16)
    r1 = x - hi.astype(F32)
    mid = r1.astype(BF16)
    lo = (r1 - mid.astype(F32)).astype(BF16)
    return jnp.dot(jnp.concatenate([hi, mid, lo], axis=1), onehot3, preferred_element_type=F32)


def _ssd_kernel(xs_ref, bm_ref, cm_ref, z_ref, dt_ref, wx_ref, wb_ref, wc_ref, bx_ref, bb_ref, bc_ref,
                dtb_ref, alog_ref, dskip_ref, bnorm_ref, *rest, rows, t_valid, n_heads, has_past):
    if has_past:
        cx_ref, cbm_ref, ccm_ref, h0_ref, y_ref, hout_ref, px_ref, pb_ref, pc_ref, h_ref, at_ref = rest
    else:
        y_ref, hout_ref, px_ref, pb_ref, pc_ref, h_ref, at_ref = rest
    g = pl.program_id(1)
    c = pl.program_id(2)
    ln = B_CHUNK
    hpg = n_heads // B_GROUPS
    gw = hpg * B_HEAD_DIM
    pad = 8

    @pl.when(c == 0)
    def _():
        if has_past:
            px_ref[0:pad, :] = cx_ref[...]
            pb_ref[0:pad, :] = cbm_ref[...]
            pc_ref[0:pad, :] = ccm_ref[...]
            h_ref[...] = h0_ref[...]
        else:
            px_ref[0:pad, :] = jnp.zeros((pad, gw), F32)
            pb_ref[0:pad, :] = jnp.zeros((pad, B_STATE), F32)
            pc_ref[0:pad, :] = jnp.zeros((pad, B_STATE), F32)
            h_ref[...] = jnp.zeros(h_ref.shape, F32)
        if rows < ln:
            for p_ref in (px_ref, pb_ref, pc_ref):
                p_ref[pad + rows:pad + ln, :] = jnp.zeros((ln - rows, p_ref.shape[1]), F32)

    def conv(raw_ref, p_ref, w_ref, b_ref):
        p_ref[pad:pad + rows, :] = raw_ref[...]
        out = b_ref[...] + p_ref[pad - 3:pad - 3 + ln, :] * w_ref[0:1, :]
        for j in range(1, B_CONV):
            out = out + p_ref[pad - 3 + j:pad - 3 + j + ln, :] * w_ref[j:j + 1, :]
        if rows == ln:
            p_ref[0:pad, :] = p_ref[ln:ln + pad, :]
        return _silu(out)

    xs = conv(xs_ref, px_ref, wx_ref, bx_ref)
    bm = conv(bm_ref, pb_ref, wb_ref, bb_ref)
    cm = conv(cm_ref, pc_ref, wc_ref, bc_ref)

    lane = lax.broadcasted_iota(jnp.int32, (ln, LANES), 1)
    rowi = lax.broadcasted_iota(jnp.int32, (ln, LANES), 0)
    if rows < ln:
        dt_raw = jnp.concatenate([dt_ref[...], jnp.zeros((ln - rows, LANES), F32)], axis=0)
    else:
        dt_raw = dt_ref[...]
    live = (lane < n_heads) & (rowi < t_valid)
    dt = jnp.where(live, jax.nn.softplus(jnp.where(live, dt_raw, 0.0) + dtb_ref[...]), 0.0)
    a_step = dt * (-jnp.exp(alog_ref[...]))
    tri = jnp.where(lax.broadcasted_iota(jnp.int32, (ln, ln), 0) >= lax.broadcasted_iota(jnp.int32, (ln, ln), 1),
                    1.0, 0.0)
    acum = jnp.dot(tri, a_step, preferred_element_type=F32, precision=lax.Precision.HIGHEST)
    at_ref[...] = acum.T

    head_of_lane = g * hpg + _div_pow2(lax.broadcasted_iota(jnp.int32, (LANES, gw), 1), B_HEAD_DIM)
    oh_ch = jnp.where(lax.broadcasted_iota(jnp.int32, (LANES, gw), 0) == head_of_lane, 1.0, 0.0).astype(BF16)
    oh_ch3 = jnp.concatenate([oh_ch, oh_ch, oh_ch], axis=0)
    head_of_blk = g * hpg + _div_pow2(lax.broadcasted_iota(jnp.int32, (LANES, hpg * ln), 1), ln)
    oh_blk = jnp.where(lax.broadcasted_iota(jnp.int32, (LANES, hpg * ln), 0) == head_of_blk, 1.0, 0.0).astype(BF16)
    oh_blk3 = jnp.concatenate([oh_blk, oh_blk, oh_blk], axis=0)
    dt_e = _expand_exact(dt, oh_ch3)
    ac_e = _expand_exact(acum, oh_ch3)
    ac_col = _expand_exact(acum, oh_blk3)
    last_e = ac_e[ln - 1:ln, :]

    xdt = xs * dt_e
    xdt_b = xdt.astype(BF16)
    w_end = (xdt * jnp.exp(last_e - ac_e)).astype(BF16)
    bm_b = bm.astype(BF16)
    cm_b = cm.astype(BF16)
    cb = lax.dot_general(cm_b, bm_b, (((1,), (1,)), ((), ())), preferred_element_type=F32)
    h_prev = h_ref[...]
    y = jnp.dot(cm_b, h_prev.astype(BF16), preferred_element_type=F32) * jnp.exp(ac_e)

    causal = lax.broadcasted_iota(jnp.int32, (ln, ln), 0) >= lax.broadcasted_iota(jnp.int32, (ln, ln), 1)
    lane_pair = lax.broadcasted_iota(jnp.int32, (ln, 2 * B_HEAD_DIM), 1)
    y_parts = []
    for pr in range(hpg // 2):
        ms = []
        for r in (2 * pr, 2 * pr + 1):
            seg = ac_col[:, r * ln:(r + 1) * ln] - at_ref[pl.ds(g * hpg + r, 1), :]
            decay = jnp.exp(jnp.where(causal, seg, -jnp.inf))
            ms.append((cb * decay).astype(BF16))
        xp = xdt_b[:, pr * 2 * B_HEAD_DIM:(pr + 1) * 2 * B_HEAD_DIM]
        zero = jnp.zeros_like(xp)
        rhs = jnp.concatenate([jnp.where(lane_pair < B_HEAD_DIM, xp, zero),
                               jnp.where(lane_pair >= B_HEAD_DIM, xp, zero)], axis=0)
        y_parts.append(jnp.dot(jnp.concatenate(ms, axis=1), rhs, preferred_element_type=F32))
    y = y + jnp.concatenate(y_parts, axis=1)

    upd = jnp.dot(bm.T.astype(BF16), w_end, preferred_element_type=F32)
    h_new = h_prev * jnp.exp(last_e) + upd
    h_ref[...] = h_new
    hout_ref[...] = h_new

    y = (y + dskip_ref[...] * xs)
    if rows < ln:
        y = y[0:rows, :]
    y = y * _silu(z_ref[...])
    y = y * lax.rsqrt(jnp.mean(y * y, axis=-1, keepdims=True) + EPS) * bnorm_ref[...]
    y_ref[...] = y.astype(y_ref.dtype)


def _ssd(proj, nb, t_rows, t_valid, xc, bc, cc, zc, dtc, layer, conv_w, conv_b, dt_bias_p, a_log_p,
         d_skip_e, b_norm, past=None):
    ln = B_CHUNK
    rows = min(ln, t_rows)
    nc = t_rows // rows
    width = d_skip_e.shape[1]
    n_heads = width // B_HEAD_DIM
    gw = width // B_GROUPS
    st = B_STATE
    xoff = 0
    boff = width // st
    coff = boff + B_GROUPS
    ne = conv_w.shape[0]
    cbias = conv_b.reshape(ne, 1, conv_b.shape[1])
    row_blk = lambda b, g, c: b * nc + c
    in_specs = [
        pl.BlockSpec((rows, gw), lambda b, g, c: (row_blk(b, g, c), xc + g)),
        pl.BlockSpec((rows, st), lambda b, g, c: (row_blk(b, g, c), bc + g)),
        pl.BlockSpec((rows, st), lambda b, g, c: (row_blk(b, g, c), cc + g)),
        pl.BlockSpec((rows, gw), lambda b, g, c: (row_blk(b, g, c), zc + g)),
        pl.BlockSpec((rows, LANES), lambda b, g, c: (row_blk(b, g, c), dtc)),
        pl.BlockSpec((None, B_CONV, gw), lambda b, g, c: (layer, 0, xoff + g)),
        pl.BlockSpec((None, B_CONV, st), lambda b, g, c: (layer, 0, boff + g)),
        pl.BlockSpec((None, B_CONV, st), lambda b, g, c: (layer, 0, coff + g)),
        pl.BlockSpec((None, 1, gw), lambda b, g, c: (layer, 0, xoff + g)),
        pl.BlockSpec((None, 1, st), lambda b, g, c: (layer, 0, boff + g)),
        pl.BlockSpec((None, 1, st), lambda b, g, c: (layer, 0, coff + g)),
        pl.BlockSpec((None, 1, LANES), lambda b, g, c: (layer, 0, 0)),
        pl.BlockSpec((None, 1, LANES), lambda b, g, c: (layer, 0, 0)),
        pl.BlockSpec((None, 1, gw), lambda b, g, c: (layer, 0, g)),
        pl.BlockSpec((None, 1, gw), lambda b, g, c: (layer, 0, g)),
    ]
    args = [proj, proj, proj, proj, proj, conv_w, conv_w, conv_w, cbias, cbias, cbias,
            dt_bias_p, a_log_p, d_skip_e.reshape(ne, 1, width), b_norm.reshape(ne, 1, width)]
    if past is not None:
        conv_hist, h0t = past
        in_specs += [
            pl.BlockSpec((None, 8, gw), lambda b, g, c: (b, 0, xoff + g)),
            pl.BlockSpec((None, 8, st), lambda b, g, c: (b, 0, boff + g)),
            pl.BlockSpec((None, 8, st), lambda b, g, c: (b, 0, coff + g)),
            pl.BlockSpec((None, st, gw), lambda b, g, c: (b, 0, g)),
        ]
        args += [conv_hist, conv_hist, conv_hist, h0t]
    y, h_t = pl.pallas_call(
        functools.partial(_ssd_kernel, rows=rows, t_valid=t_valid, n_heads=n_heads, has_past=past is not None),
        out_shape=[jax.ShapeDtypeStruct((nb * t_rows, width), BF16),
                   jax.ShapeDtypeStruct((nb, st, width), F32)],
        grid=(nb, B_GROUPS, nc),
        in_specs=in_specs,
        out_specs=[pl.BlockSpec((rows, gw), lambda b, g, c: (row_blk(b, g, c), g)),
                   pl.BlockSpec((None, st, gw), lambda b, g, c: (b, 0, g))],
        scratch_shapes=[pltpu.VMEM((ln + 8, gw), F32), pltpu.VMEM((ln + 8, st), F32),
                        pltpu.VMEM((ln + 8, st), F32), pltpu.VMEM((st, gw), F32),
                        pltpu.VMEM((LANES, ln), F32)],
        compiler_params=_params(("arbitrary", "arbitrary", "arbitrary"), VMEM_MID),
        name="ssd",
    )(*args)
    return y, h_t


def _shift_kernel(a_ref, nxt_ref, new_ref, o_ref, *, blk, n_blocks):
    g = pl.program_id(1)
    o_ref[0:blk - 1] = a_ref[1:blk]

    @pl.when(g < n_blocks - 1)
    def _():
        o_ref[blk - 1] = nxt_ref[0]

    @pl.when(g == n_blocks - 1)
    def _():
        o_ref[blk - 1] = new_ref[0]


def _shift_window(cache, new_rows, blk=64):
    e, b, past, heads, hd = cache.shape
    dec = new_rows.shape[2]
    assert past % (dec * blk) == 0
    grp = dec * heads
    n_grp = past // dec
    n_blocks = n_grp // blk
    c4 = cache.reshape(e * b, n_grp, grp, hd)
    n4 = new_rows.reshape(e * b, 1, grp, hd)
    out = pl.pallas_call(
        functools.partial(_shift_kernel, blk=blk, n_blocks=n_blocks),
        out_shape=jax.ShapeDtypeStruct(c4.shape, cache.dtype),
        grid=(e * b, n_blocks),
        in_specs=[pl.BlockSpec((None, blk, grp, hd), lambda n, g: (n, g, 0, 0)),
                  pl.BlockSpec((None, 1, grp, hd), lambda n, g: (n, jnp.minimum((g + 1) * blk, n_grp - 1), 0, 0)),
                  pl.BlockSpec((None, 1, grp, hd), lambda n, g: (n, 0, 0, 0))],
        out_specs=pl.BlockSpec((None, blk, grp, hd), lambda n, g: (n, g, 0, 0)),
        compiler_params=_params(("arbitrary", "arbitrary"), VMEM_MID),
        name="window_shift",
    )(c4, c4, n4)
    return out.reshape(cache.shape)


def _alibi_slopes(n_heads):
    return jnp.exp2(-8.0 * jnp.arange(1, n_heads + 1, dtype=F32) / n_heads)


def _pad_lanes(v):
    return jnp.pad(v, ((0, 0), (0, LANES - v.shape[1]))).reshape(v.shape[0], 1, LANES)


def kernel(x_prompt, x_sample, cache_win_k, cache_win_v, state_conv, state_ssm, cache_diff_k, cache_diff_v,
           page_table, c_prompt, c_sample, ln_g, ln_b, w_mod, b_mod, w_in_even, w_out_even, conv_w, conv_b,
           dt_bias, a_log, d_skip, b_norm, w_in_odd, w_out_odd, lam_q1, lam_k1, lam_q2, lam_k2, c_norm):
    nbp, seq, d = x_prompt.shape
    nbs, dec_seq, _ = x_sample.shape
    depth = w_mod.shape[0]
    alpha = (2 * depth) ** 0.25
    a_heads = cache_win_k.shape[3]
    a_width = a_heads * A_HEAD_DIM
    c_heads = cache_diff_k.shape[3]
    b_width = b_norm.shape[1]
    conv_dim = conv_w.shape[2]
    b_heads = dt_bias.shape[1]
    assert dec_seq <= DEC_PAD and seq % B_CHUNK == 0

    slopes_a = _alibi_slopes(a_heads)
    slopes_c = _alibi_slopes(c_heads)

    c_rows = jnp.concatenate([c_prompt, c_sample], axis=0)
    c_rows = jnp.pad(c_rows, ((0, (-c_rows.shape[0]) % 8), (0, 0)))
    mods = _mods(c_rows, w_mod, b_mod)
    mods_p = mods[:, :nbp].reshape(depth, nbp, 1, 3 * d)
    mods_s = jnp.repeat(mods[:, nbp:nbp + nbs], DEC_PAD, axis=1).reshape(depth, 1, nbs * DEC_PAD, 3 * d)

    xp = x_prompt.reshape(nbp * seq, d)
    xs = jnp.pad(x_sample, ((0, 0), (0, DEC_PAD - dec_seq), (0, 0))).reshape(nbs * DEC_PAD, d)
    ms_rows = nbs * DEC_PAD
    tm_row = 256

    hp = _modulate(xp, mods_p, 0, tm_row, seq)
    hs = _modulate(xs, mods_s, 0, ms_rows, ms_rows)

    dt_bias_p, a_log_p = _pad_lanes(dt_bias), _pad_lanes(a_log)
    d_skip_e = jnp.repeat(d_skip, B_HEAD_DIM, axis=1)
    conv_hist = jnp.pad(state_conv, ((0, 0), (0, 0), (8 - (B_CONV - 1), 0), (0, 0)))
    h0t = state_ssm.transpose(0, 1, 4, 2, 3).reshape(state_ssm.shape[0], nbs, B_STATE, b_width)

    def rows_th(a, width):
        return a.reshape(nbs, DEC_PAD * (a.shape[1] // width), width)

    n_even, n_odd = w_in_even.shape[0], w_in_odd.shape[0]
    past_win = cache_win_k.shape[2]
    assert seq <= A_PATTERNS[-1][0] and past_win == A_PATTERNS[-1][0]
    wk_p = wv_p = dk_p = dv_p = None
    outs = {k: [] for k in ("cv_p", "ss_p", "wk_s", "wv_s", "cv_s", "ss_s", "dk_s", "dv_s")}
    for l in range(depth):
        last = l == depth - 1
        if l % 2 == 0:
            e = l // 2
            gw = b_width // B_GROUPS
            r0 = 3 * a_width
            z0, x0 = a_width, a_width + b_width
            ssd_cols = dict(xc=x0 // gw, bc=(x0 + b_width) // LANES, cc=(x0 + b_width) // LANES + B_GROUPS,
                            zc=z0 // gw, dtc=(x0 + conv_dim) // LANES)
            ssd_w = dict(layer=e, conv_w=conv_w, conv_b=conv_b, dt_bias_p=dt_bias_p, a_log_p=a_log_p,
                         d_skip_e=d_skip_e, b_norm=b_norm)
            aq_p = _mm([hp], w_in_even, e, 1024, 0, a_width)
            wk_p = _mm([hp], w_in_even, e, 1024, a_width, a_width, stack=(wk_p, n_even))
            wv_p = _mm([hp], w_in_even, e, 1024, 2 * a_width, a_width, stack=(wv_p, n_even))
            rest_p = _mm([hp], w_in_even, e, 1024, r0)
            ya_p = _dil_prompt(aq_p, wk_p, wv_p, rest_p, e, slopes_a, nbp, seq, a_heads, 0)
            yb_p, ht_p = _ssd(rest_p, nbp, seq, seq, **ssd_cols, **ssd_w)
            yp = _mm([ya_p, yb_p], w_out_even, e, 512)
            outs["cv_p"].append(rest_p.reshape(nbp, seq, -1)[:, seq - (B_CONV - 1):, x0:x0 + conv_dim])
            outs["ss_p"].append(ht_p.reshape(nbp, B_STATE, b_heads, B_HEAD_DIM).transpose(0, 2, 3, 1))
            ps = _mm([hs], w_in_even, e, ms_rows)
            q_s = rows_th(ps[:, 0:a_width], A_HEAD_DIM)
            k_s = rows_th(ps[:, a_width:2 * a_width], A_HEAD_DIM)
            v_s = rows_th(ps[:, 2 * a_width:3 * a_width], A_HEAD_DIM)
            g_s = rows_th(ps[:, 3 * a_width:4 * a_width], A_HEAD_DIM)
            ya_s = _dil_decode(q_s, g_s, k_s, v_s, cache_win_k, cache_win_v, e, slopes_a, a_heads)
            ya_s = ya_s.reshape(ms_rows, a_width)
            yb_s, ht_s = _ssd(ps[:, r0:], nbs, DEC_PAD, dec_seq, past=(conv_hist[e], h0t[e]), **ssd_cols, **ssd_w)
            ys = _mm([ya_s, yb_s], w_out_even, e, ms_rows)
            ps3 = ps.reshape(nbs, DEC_PAD, -1)
            outs["wk_s"].append(ps3[:, :dec_seq, a_width:2 * a_width].reshape(nbs, dec_seq, a_heads, A_HEAD_DIM))
            outs["wv_s"].append(ps3[:, :dec_seq, 2 * a_width:3 * a_width].reshape(nbs, dec_seq, a_heads, A_HEAD_DIM))
            xbc_all = jnp.concatenate([state_conv[e], ps3[:, :dec_seq, r0 + x0:r0 + x0 + conv_dim]], axis=1)
            outs["cv_s"].append(xbc_all[:, -(B_CONV - 1):])
            outs["ss_s"].append(ht_s.reshape(nbs, B_STATE, b_heads, B_HEAD_DIM).transpose(0, 2, 3, 1))
        else:
            o = l // 2
            lam_init = 0.8 - 0.6 * math.exp(-0.3 * l)
            lam = (jnp.exp(jnp.sum(lam_q1[o] * lam_k1[o])) - jnp.exp(jnp.sum(lam_q2[o] * lam_k2[o])) + lam_init)
            scalars = jnp.concatenate([slopes_c, lam.reshape(1)])
            qkw = c_heads * 2 * C_QK_DIM
            vw = c_heads * C_V_DIM
            q_p = _mm([hp], w_in_odd, o, 1024, 0, qkw)
            dk_p = _mm([hp], w_in_odd, o, 1024, qkw, qkw, stack=(dk_p, n_odd))
            dv_p = _mm([hp], w_in_odd, o, 1024, 2 * qkw, vw, stack=(dv_p, n_odd))
            g_p = _mm([hp], w_in_odd, o, 1024, 2 * qkw + vw, vw)
            y_p = _diff_prompt(q_p, dk_p, dv_p, g_p, o, scalars, c_norm, nbp, seq, c_heads, lam_init)
            yp = _mm([y_p], w_out_odd, o, 1024)
            ps = _mm([hs], w_in_odd, o, ms_rows)
            q_s = rows_th(ps[:, 0:qkw], 2 * C_QK_DIM)
            k_s = rows_th(ps[:, qkw:2 * qkw], 2 * C_QK_DIM)
            v_s = rows_th(ps[:, 2 * qkw:2 * qkw + vw], C_V_DIM)
            g_s = rows_th(ps[:, 2 * qkw + vw:], C_V_DIM)
            y_s = _diff_decode(q_s, g_s, k_s, v_s, cache_diff_k, cache_diff_v, page_table, o, scalars, c_norm,
                               c_heads, lam_init)
            ys = _mm([y_s.reshape(ms_rows, vw)], w_out_odd, o, ms_rows)
            ps3 = ps.reshape(nbs, DEC_PAD, -1)
            outs["dk_s"].append(ps3[:, :dec_seq, qkw:2 * qkw].reshape(nbs, dec_seq, c_heads, 2 * C_QK_DIM))
            outs["dv_s"].append(ps3[:, :dec_seq, 2 * qkw:2 * qkw + vw].reshape(nbs, dec_seq, c_heads, C_V_DIM))
        xp, hp = _post(xp, yp, mods_p, l, ln_g, ln_b, alpha, tm_row, seq, not last)
        xs, hs = _post(xs, ys, mods_s, l, ln_g, ln_b, alpha, ms_rows, ms_rows, not last)

    y_prompt = xp.reshape(nbp, seq, d)
    y_sample = xs.reshape(nbs, DEC_PAD, d)[:, :dec_seq]
    st = lambda k: jnp.stack(outs[k])
    wk_s = _shift_window(cache_win_k, st("wk_s"))
    wv_s = _shift_window(cache_win_v, st("wv_s"))
    return (y_prompt, y_sample,
            wk_p.reshape(n_even, nbp, seq, a_heads, A_HEAD_DIM), wv_p.reshape(n_even, nbp, seq, a_heads, A_HEAD_DIM),
            st("cv_p"), st("ss_p"),
            dk_p.reshape(n_odd, nbp, seq, c_heads, 2 * C_QK_DIM), dv_p.reshape(n_odd, nbp, seq, c_heads, C_V_DIM),
            wk_s, wv_s, st("cv_s"), st("ss_s"), st("dk_s"), st("dv_s"))
```

```python
import functools
import math

import jax
import jax.numpy as jnp
from jax import lax
from jax.experimental import pallas as pl
from jax.experimental.pallas import tpu as pltpu

F32 = jnp.float32
BF16 = jnp.bfloat16
EPS = 1e-5
NEG = -1e30
A_PATTERNS = ((128, 1), (512, 4), (2048, 16))
A_HEAD_DIM = 128
B_HEAD_DIM = 64
B_GROUPS = 8
B_STATE = 128
B_CONV = 4
B_CHUNK = 128
C_QK_DIM = 128
C_V_DIM = 256
DEC_PAD = 8
LANES = 128
VMEM_BIG = 48 * 1024 * 1024
VMEM_MID = 32 * 1024 * 1024


def _params(sem, vmem=None):
    return pltpu.CompilerParams(dimension_semantics=sem, vmem_limit_bytes=vmem)


def _silu(x):
    return x * jax.nn.sigmoid(x)


def _div_pow2(x, n):
    assert n & (n - 1) == 0
    return x >> (n.bit_length() - 1)


def _mods_kernel(c_ref, w_ref, b_ref, o_ref, *, kc):
    c = c_ref[...]
    s = _silu(c).astype(BF16)
    d = c.shape[1]
    acc = jnp.zeros(o_ref.shape, F32)
    for k0 in range(0, d, kc):
        acc = acc + jnp.dot(s[:, k0:k0 + kc], w_ref[k0:k0 + kc, :].astype(BF16),
                            preferred_element_type=F32)
    o_ref[...] = acc + b_ref[...]


def _mods(c_rows, w_mod, b_mod, tn=512):
    nl, d, n = w_mod.shape
    r = c_rows.shape[0]
    return pl.pallas_call(
        functools.partial(_mods_kernel, kc=512),
        out_shape=jax.ShapeDtypeStruct((nl, r, n), F32),
        grid=(nl, n // tn),
        in_specs=[pl.BlockSpec((r, d), lambda l, j: (0, 0)),
                  pl.BlockSpec((None, d, tn), lambda l, j: (l, 0, j)),
                  pl.BlockSpec((None, 1, tn), lambda l, j: (l, 0, j))],
        out_specs=pl.BlockSpec((None, r, tn), lambda l, j: (l, 0, j)),
        compiler_params=_params(("arbitrary", "arbitrary"), VMEM_MID),
        name="mods",
    )(c_rows, w_mod, b_mod.reshape(nl, 1, n))


def _mm_kernel(*refs, k_sizes, n_extra, cast_rows, w_t):
    nx = len(k_sizes)
    x_refs, e_refs, w_ref = refs[:nx], refs[nx:nx + n_extra], refs[nx + n_extra]
    wb_ref = refs[-1]
    o_ref = refs[-3] if n_extra else refs[-2]

    def product(part_refs):
        acc = None
        off = 0
        for x_ref, ks in zip(part_refs, k_sizes):
            if w_t:
                part = lax.dot_general(x_ref[...], wb_ref[:, off:off + ks], (((1,), (1,)), ((), ())),
                                       preferred_element_type=F32)
            else:
                part = jnp.dot(x_ref[...], wb_ref[off:off + ks, :], preferred_element_type=F32)
            acc = part if acc is None else acc + part
            off += ks
        return acc

    @pl.when(pl.program_id(1) == 0)
    def _():
        def body(r, carry):
            rows = pl.ds(pl.multiple_of(r * cast_rows, cast_rows), cast_rows)
            wb_ref[rows, :] = w_ref[rows, :].astype(BF16)
            return carry
        lax.fori_loop(0, wb_ref.shape[0] // cast_rows, body, 0)
        if n_extra:
            refs[-2][...] = product(e_refs)

    o_ref[...] = product(x_refs)


def _mm(x_parts, w, layer, tm, col0=0, ncols=None, stack=None, extra=None, w_t=False, tn=512):
    m = x_parts[0].shape[0]
    k_sizes = tuple(x.shape[1] for x in x_parts)
    k = sum(k_sizes)
    n = w.shape[1] if w_t else w.shape[2]
    ncols = n - col0 if ncols is None else ncols
    cast_rows = 64 if w_t else 256
    assert (w.shape[2] if w_t else w.shape[1]) == k and m % tm == 0 and k % 256 == 0 and col0 % tn == 0
    cb0 = col0 // tn
    in_specs = [pl.BlockSpec((tm, ks), lambda j, i: (i, 0)) for ks in k_sizes]
    args = list(x_parts)
    out_shape, out_specs = [], []
    if extra is not None:
        me = extra[0].shape[0]
        assert tuple(x.shape[1] for x in extra) == k_sizes
        in_specs += [pl.BlockSpec((me, ks), lambda j, i: (0, 0)) for ks in k_sizes]
        args += list(extra)
    if w_t:
        in_specs.append(pl.BlockSpec((None, tn, k), lambda j, i: (layer, cb0 + j, 0)))
    else:
        in_specs.append(pl.BlockSpec((None, k, tn), lambda j, i: (layer, 0, cb0 + j)))
    args.append(w)
    aliases = {}
    if stack is None:
        out_shape.append(jax.ShapeDtypeStruct((m, ncols), F32))
        out_specs.append(pl.BlockSpec((tm, tn), lambda j, i: (i, j)))
    else:
        buf, depth = stack
        assert ncols % tn == 0
        out_shape.append(jax.ShapeDtypeStruct((depth, m, ncols), F32))
        out_specs.append(pl.BlockSpec((None, tm, tn), lambda j, i: (layer, i, j)))
        if buf is not None:
            in_specs.append(pl.BlockSpec(memory_space=pl.ANY))
            args.append(buf)
            aliases = {len(args) - 1: 0}
    if extra is not None:
        out_shape.append(jax.ShapeDtypeStruct((me, ncols), F32))
        out_specs.append(pl.BlockSpec((me, tn), lambda j, i: (0, j)))
    res = pl.pallas_call(
        functools.partial(_mm_kernel, k_sizes=k_sizes, n_extra=0 if extra is None else len(extra),
                          cast_rows=cast_rows, w_t=w_t),
        out_shape=out_shape,
        grid=(pl.cdiv(ncols, tn), m // tm),
        in_specs=in_specs,
        out_specs=out_specs,
        scratch_shapes=[pltpu.VMEM((tn, k) if w_t else (k, tn), BF16)],
        input_output_aliases=aliases,
        compiler_params=_params(("arbitrary", "arbitrary"), VMEM_BIG),
        name="proj_mm",
    )(*args)
    return (res[0], res[1]) if extra is not None else res[0]


def _modulate_kernel(x_ref, sh_ref, sc_ref, h_ref):
    h_ref[...] = (x_ref[...] * (1.0 + sc_ref[...]) + sh_ref[...]).astype(h_ref.dtype)


def _post_kernel(x_ref, y_ref, gt_ref, g_ref, b_ref, *rest, alpha, with_next):
    if with_next:
        sh_ref, sc_ref, xo_ref, h_ref = rest
    else:
        (xo_ref,) = rest
    r = alpha * x_ref[...] + (1.0 + gt_ref[...]) * y_ref[...]
    mu = jnp.mean(r, axis=-1, keepdims=True)
    rc = r - mu
    var = jnp.mean(rc * rc, axis=-1, keepdims=True)
    xn = rc * lax.rsqrt(var + EPS) * g_ref[...] + b_ref[...]
    xo_ref[...] = xn
    if with_next:
        h_ref[...] = (xn * (1.0 + sc_ref[...]) + sh_ref[...]).astype(h_ref.dtype)


def _mod_spec(mods, layer, part, tm, rows_per_seq):
    d = mods.shape[3] // 3
    r = mods.shape[2]
    if r == 1:
        tiles = rows_per_seq // tm
        return pl.BlockSpec((None, None, 1, d), lambda i: (layer, i // tiles, 0, part))
    return pl.BlockSpec((None, None, r, d), lambda i: (layer, 0, i, part))


def _modulate(x, mods, layer, tm, rows_per_seq):
    m, d = x.shape
    row = pl.BlockSpec((tm, d), lambda i: (i, 0))
    return pl.pallas_call(
        _modulate_kernel,
        out_shape=jax.ShapeDtypeStruct((m, d), BF16),
        grid=(m // tm,),
        in_specs=[row, _mod_spec(mods, layer, 0, tm, rows_per_seq),
                  _mod_spec(mods, layer, 1, tm, rows_per_seq)],
        out_specs=row,
        compiler_params=_params(("arbitrary",), VMEM_MID),
        name="modulate",
    )(x, mods, mods)


def _post(x, y, mods, layer, ln_g, ln_b, alpha, tm, rows_per_seq, with_next):
    m, d = x.shape
    nl = ln_g.shape[0]
    row = pl.BlockSpec((tm, d), lambda i: (i, 0))
    vec = pl.BlockSpec((None, 1, d), lambda i: (layer, 0, 0))
    in_specs = [row, row, _mod_spec(mods, layer, 2, tm, rows_per_seq), vec, vec]
    args = [x, y, mods, ln_g.reshape(nl, 1, d), ln_b.reshape(nl, 1, d)]
    out_shape = [jax.ShapeDtypeStruct((m, d), F32)]
    out_specs = [row]
    if with_next:
        in_specs += [_mod_spec(mods, layer + 1, 0, tm, rows_per_seq),
                     _mod_spec(mods, layer + 1, 1, tm, rows_per_seq)]
        args += [mods, mods]
        out_shape.append(jax.ShapeDtypeStruct((m, d), BF16))
        out_specs.append(row)
    res = pl.pallas_call(
        functools.partial(_post_kernel, alpha=alpha, with_next=with_next),
        out_shape=out_shape,
        grid=(m // tm,),
        in_specs=in_specs,
        out_specs=out_specs,
        compiler_params=_params(("arbitrary",), VMEM_MID),
        name="post_ln",
    )(*args)
    return (res[0], res[1]) if with_next else (res[0], None)


def _pattern_count(dist):
    cnt = jnp.zeros(dist.shape, F32)
    for window, dil in A_PATTERNS:
        ok = (dist >= 0) & (dist <= window) & ((dist & (dil - 1)) == 0)
        cnt = cnt + jnp.where(ok, 1.0, 0.0)
    return cnt


M_FLOOR = -1e20
LOG2E = math.log2(math.e)


def _flash_t_tile(k_maps, v_t, q_maps, bias_t, cj, scale, stats):
    ps, alphas = [], []
    for k, q, (m_ref, l_ref, _) in zip(k_maps, q_maps, stats):
        s = lax.dot_general(k, q, (((1,), (1,)), ((), ())), preferred_element_type=F32) * (scale * LOG2E) + bias_t
        m_prev = m_ref[...]
        m_new = jnp.maximum(m_prev, jnp.max(s, axis=0, keepdims=True) - cj)
        p = jnp.exp2(s - (m_new + cj))
        alpha = jnp.exp2(m_prev - m_new)
        l_ref[...] = alpha * l_ref[...] + jnp.sum(p, axis=0, keepdims=True)
        m_ref[...] = m_new
        ps.append(p.astype(BF16))
        alphas.append(alpha)
    pv = jnp.dot(v_t, ps[0] if len(ps) == 1 else jnp.concatenate(ps, axis=1), preferred_element_type=F32)
    tq = ps[0].shape[1]
    for n, (alpha, (_, _, acc_ref)) in enumerate(zip(alphas, stats)):
        acc_ref[...] = alpha * acc_ref[...] + pv[:, n * tq:(n + 1) * tq]


def _flash_t_diag(k_of, vt_ref, row0, q_of, bias_diag, scale, stats):
    tile = bias_diag.shape[0]
    half = tile // 2
    early = pl.ds(row0, half)
    late = pl.ds(row0 + half, half)
    _flash_t_tile(k_of(early), vt_ref[:, early], q_of(0), bias_diag[0:half, :], 0.0, scale, stats)
    hi = tuple(tuple(r.at[:, half:] for r in st) for st in stats)
    _flash_t_tile(k_of(late), vt_ref[:, late], q_of(half), bias_diag[half:, half:], 0.0, scale, hi)


def _flash_t_reset(stats):
    for m_ref, l_ref, acc_ref in stats:
        m_ref[...] = jnp.full(m_ref.shape, M_FLOOR, F32)
        l_ref[...] = jnp.zeros(l_ref.shape, F32)
        acc_ref[...] = jnp.zeros(acc_ref.shape, F32)


def _stage_kv(k_ref, v_ref, kb_ref, vt_ref):
    t = k_ref.shape[0]
    for r0 in range(0, t, 256):
        kb_ref[r0:r0 + 256, :] = k_ref[r0:r0 + 256, :].astype(BF16)
    for r0 in range(0, t, LANES):
        vt_ref[:, r0:r0 + LANES] = v_ref[r0:r0 + LANES, :].T.astype(BF16)


def _dil_prompt_kernel(slopes_ref, q_ref, k_ref, v_ref, g_ref, logc_ref, o_ref,
                       kb_ref, vt_ref, bias_ref, m_ref, l_ref, acc_ref, *, tile, window):
    h = pl.program_id(1)
    i = pl.program_id(2)
    slope = slopes_ref[h]
    n_off = bias_ref.shape[0]

    @pl.when(i == 0)
    def _():
        _stage_kv(k_ref, v_ref, kb_ref, vt_ref)
        rel_t = (lax.broadcasted_iota(jnp.int32, (tile, tile), 1)
                 - lax.broadcasted_iota(jnp.int32, (tile, tile), 0)).astype(F32)
        for d in range(n_off):
            bias_ref[d] = (logc_ref[d] - slope * rel_t) * LOG2E

    stats = ((m_ref, l_ref, acc_ref),)
    _flash_t_reset(stats)
    scale = A_HEAD_DIM ** -0.5
    q = q_ref[...].astype(BF16)

    def body(j, carry):
        rows = pl.ds(pl.multiple_of(j * tile, tile), tile)
        cj = (slope * LOG2E) * ((i - j) * tile).astype(F32)
        _flash_t_tile((kb_ref[rows, :],), vt_ref[:, rows], (q,), bias_ref[i - j], cj, scale, stats)
        return carry

    lax.fori_loop(jnp.maximum(i - (n_off - 1), 0), i, body, 0)
    _flash_t_diag(lambda rows: (kb_ref[rows, :],), vt_ref, pl.multiple_of(i * tile, tile),
                  lambda lo: (q[lo:, :],), bias_ref.at[0], scale, stats)
    o = (acc_ref[...] * (1.0 / l_ref[...])).T
    o_ref[...] = (o * _silu(g_ref[...])).astype(o_ref.dtype)


def _log_count_tiles(tile, n_off):
    import numpy as np
    dist = (np.arange(n_off)[:, None, None] * tile + np.arange(tile)[None, None, :] - np.arange(tile)[None, :, None])
    cnt = np.zeros(dist.shape, np.float64)
    for window, dil in A_PATTERNS:
        cnt += (dist >= 0) & (dist <= window) & (dist % dil == 0)
    return jnp.asarray(np.where(cnt > 0, np.log(np.maximum(cnt, 1.0)), NEG), F32)


def _dil_prompt(q, k_stack, v_stack, g, layer, slopes, nb, t, heads, gc, tile=512):
    hd = A_HEAD_DIM
    nq = t // tile
    window = A_PATTERNS[-1][0]
    n_off = min(nq, window // tile + 2)
    logc = _log_count_tiles(tile, n_off)
    return pl.pallas_call(
        functools.partial(_dil_prompt_kernel, tile=tile, window=window),
        out_shape=jax.ShapeDtypeStruct((nb * t, heads * hd), BF16),
        grid=(nb, heads, nq),
        in_specs=[pl.BlockSpec(memory_space=pltpu.SMEM),
                  pl.BlockSpec((tile, hd), lambda b, h, i: (b * nq + i, h)),
                  pl.BlockSpec((None, t, hd), lambda b, h, i: (layer, b, h)),
                  pl.BlockSpec((None, t, hd), lambda b, h, i: (layer, b, h)),
                  pl.BlockSpec((tile, hd), lambda b, h, i: (b * nq + i, gc + h)),
                  pl.BlockSpec((n_off, tile, tile), lambda b, h, i: (0, 0, 0))],
        out_specs=pl.BlockSpec((tile, hd), lambda b, h, i: (b * nq + i, h)),
        scratch_shapes=[pltpu.VMEM((t, hd), BF16), pltpu.VMEM((hd, t), BF16),
                        pltpu.VMEM((n_off, tile, tile), F32),
                        pltpu.VMEM((1, tile), F32), pltpu.VMEM((1, tile), F32), pltpu.VMEM((hd, tile), F32)],
        compiler_params=_params(("arbitrary", "arbitrary", "arbitrary"), VMEM_BIG),
        name="dilated_prompt",
    )(slopes, q, k_stack, v_stack, g, logc)


def _dil_decode_kernel(slopes_ref, q_ref, g_ref, kc_ref, vc_ref, kn_ref, vn_ref, tab_ref, tabn_ref, o_ref,
                       vt_ref, base_ref, slope_ref, m_ref, l_ref, acc_ref, *, heads, past, kb):
    c = pl.program_id(1)
    nc = past * heads // kb
    rq = q_ref.shape[0]
    stats = ((m_ref, l_ref, acc_ref),)

    def token_offset(nk):
        key = lax.broadcasted_iota(jnp.int32, (nk, rq), 0)
        qry = lax.broadcasted_iota(jnp.int32, (nk, rq), 1)
        return (_div_pow2(qry, heads) - _div_pow2(key, heads)).astype(F32)

    @pl.when(c == 0)
    def _():
        _flash_t_reset(stats)
        qh = lax.broadcasted_iota(jnp.int32, (1, rq), 1) & (heads - 1)
        slope = jnp.zeros((1, rq), F32)
        for hh in range(heads):
            slope = jnp.where(qh == hh, slopes_ref[hh], slope)
        slope_ref[...] = slope * LOG2E
        base_ref[...] = -(slope * LOG2E) * token_offset(kb)

    q = (q_ref[...].astype(BF16),)

    def attend(k_ref, v_ref, bias_t, shift):
        nk = k_ref.shape[0]
        for r0 in range(0, nk, LANES):
            vt_ref[:, r0:r0 + LANES] = v_ref[r0:r0 + LANES, :].T.astype(BF16)
        _flash_t_tile((k_ref[...].astype(BF16),), vt_ref[:, 0:nk], q, bias_t, shift, A_HEAD_DIM ** -0.5, stats)

    attend(kc_ref, vc_ref, tab_ref[...] + base_ref[...],
           slope_ref[...] * (past - c * (kb // heads)).astype(F32))

    @pl.when(c == nc - 1)
    def _():
        nn = kn_ref.shape[0]
        attend(kn_ref, vn_ref, tabn_ref[...] - slope_ref[...] * token_offset(nn), jnp.zeros((1, rq), F32))
        o = (acc_ref[...] * (1.0 / l_ref[...])).T
        o_ref[...] = (o * _silu(g_ref[...])).astype(o_ref.dtype)


def _decode_count_tables(heads, past, kb, rq, n_new):
    per = kb // heads
    q_tok = jnp.arange(rq) // heads
    q_head = jnp.arange(rq) % heads

    def table(key_pos, key_head):
        dist = (past + q_tok)[None, :] - key_pos[:, None]
        cnt = _pattern_count(dist)
        ok = (key_head[:, None] == q_head[None, :]) & (cnt > 0)
        return jnp.where(ok, jnp.log(jnp.maximum(cnt, 1.0)) * LOG2E, NEG)

    slot = jnp.arange(kb)
    chunks = jnp.stack([table(c * per + slot // heads, slot % heads) for c in range(past * heads // kb)])
    new = jnp.arange(n_new)
    return chunks, table(past + new // heads, new % heads)


def _dil_decode(q, g, k_new, v_new, cache_k, cache_v, layer, slopes, heads, kb=2048):
    nb, rq, hd = q.shape
    past = cache_k.shape[2]
    nn = k_new.shape[1]
    assert rq % LANES == 0 and nn <= kb
    ck = cache_k.reshape(cache_k.shape[0], nb, past * heads, hd)
    cv = cache_v.reshape(cache_v.shape[0], nb, past * heads, hd)
    tab, tab_new = _decode_count_tables(heads, past, kb, rq, nn)
    small = pl.BlockSpec((None, rq, hd), lambda b, c: (b, 0, 0))
    cache = pl.BlockSpec((None, None, kb, hd), lambda b, c: (layer, b, c, 0))
    return pl.pallas_call(
        functools.partial(_dil_decode_kernel, heads=heads, past=past, kb=kb),
        out_shape=jax.ShapeDtypeStruct((nb, rq, hd), BF16),
        grid=(nb, past * heads // kb),
        in_specs=[pl.BlockSpec(memory_space=pltpu.SMEM), small, small, cache, cache, small, small,
                  pl.BlockSpec((None, kb, rq), lambda b, c: (c, 0, 0)),
                  pl.BlockSpec((nn, rq), lambda b, c: (0, 0))],
        out_specs=small,
        scratch_shapes=[pltpu.VMEM((hd, kb), BF16), pltpu.VMEM((kb, rq), F32), pltpu.VMEM((1, rq), F32),
                        pltpu.VMEM((1, rq), F32), pltpu.VMEM((1, rq), F32), pltpu.VMEM((hd, rq), F32)],
        compiler_params=_params(("arbitrary", "arbitrary"), VMEM_MID),
        name="dilated_decode",
    )(slopes, q, g, ck, cv, k_new, v_new, tab, tab_new)


def _diff_prompt_kernel(sc_ref, q_ref, k_ref, v_ref, g_ref, cn_ref, o_ref, kb_ref, vt_ref, bias_ref,
                        m1_ref, l1_ref, a1_ref, m2_ref, l2_ref, a2_ref, *, tile, heads, lam_init):
    h = pl.program_id(1)
    i = pl.program_id(2)
    slope = sc_ref[h]
    lam = sc_ref[heads]
    dk = C_QK_DIM

    @pl.when(i == 0)
    def _():
        _stage_kv(k_ref, v_ref, kb_ref, vt_ref)
        rel_t = (lax.broadcasted_iota(jnp.int32, (tile, tile), 1)
                 - lax.broadcasted_iota(jnp.int32, (tile, tile), 0))
        below = (-slope * LOG2E) * rel_t.astype(F32)
        bias_ref[0] = jnp.where(rel_t >= 0, below, NEG)
        bias_ref[1] = below

    stats = ((m1_ref, l1_ref, a1_ref), (m2_ref, l2_ref, a2_ref))
    _flash_t_reset(stats)
    qs = (q_ref[:, :dk].astype(BF16), q_ref[:, dk:].astype(BF16))

    def tile_step(j, bias_t):
        rows = pl.ds(pl.multiple_of(j * tile, tile), tile)
        cj = (slope * LOG2E) * ((i - j) * tile).astype(F32)
        _flash_t_tile((kb_ref[rows, :dk], kb_ref[rows, dk:]), vt_ref[:, rows], qs, bias_t, cj,
                      C_QK_DIM ** -0.5, stats)

    def body(j, carry):
        tile_step(j, bias_ref[1])
        return carry

    lax.fori_loop(0, i, body, 0)
    _flash_t_diag(lambda rows: (kb_ref[rows, :dk], kb_ref[rows, dk:]), vt_ref, pl.multiple_of(i * tile, tile),
                  lambda lo: tuple(qm[lo:, :] for qm in qs), bias_ref.at[0], C_QK_DIM ** -0.5, stats)
    o_t = a1_ref[...] * (1.0 / l1_ref[...]) - lam * (a2_ref[...] * (1.0 / l2_ref[...]))
    o_t = o_t * lax.rsqrt(jnp.mean(o_t * o_t, axis=0, keepdims=True) + EPS)
    cn = cn_ref[...]
    o_t = o_t * jnp.concatenate([cn] * (tile // LANES), axis=1) * (1.0 - lam_init)
    o_ref[...] = (o_t.T * _silu(g_ref[...])).astype(o_ref.dtype)


def _diff_prompt(q, k_stack, v_stack, g, layer, scalars, c_norm, nb, t, heads, lam_init, tile=512):
    dv = C_V_DIM
    nq = t // tile
    cn = jnp.broadcast_to(c_norm[:, :, None], (c_norm.shape[0], dv, LANES))
    return pl.pallas_call(
        functools.partial(_diff_prompt_kernel, tile=tile, heads=heads, lam_init=lam_init),
        out_shape=jax.ShapeDtypeStruct((nb * t, heads * dv), BF16),
        grid=(nb, heads, nq),
        in_specs=[pl.BlockSpec(memory_space=pltpu.SMEM),
                  pl.BlockSpec((tile, 2 * C_QK_DIM), lambda b, h, i: (b * nq + i, h)),
                  pl.BlockSpec((None, t, 2 * C_QK_DIM), lambda b, h, i: (layer, b, h)),
                  pl.BlockSpec((None, t, dv), lambda b, h, i: (layer, b, h)),
                  pl.BlockSpec((tile, dv), lambda b, h, i: (b * nq + i, h)),
                  pl.BlockSpec((None, dv, LANES), lambda b, h, i: (layer, 0, 0))],
        out_specs=pl.BlockSpec((tile, dv), lambda b, h, i: (b * nq + i, h)),
        scratch_shapes=[pltpu.VMEM((t, 2 * C_QK_DIM), BF16), pltpu.VMEM((dv, t), BF16),
                        pltpu.VMEM((2, tile, tile), F32),
                        pltpu.VMEM((1, tile), F32), pltpu.VMEM((1, tile), F32), pltpu.VMEM((dv, tile), F32),
                        pltpu.VMEM((1, tile), F32), pltpu.VMEM((1, tile), F32), pltpu.VMEM((dv, tile), F32)],
        compiler_params=_params(("arbitrary", "arbitrary", "arbitrary"), VMEM_BIG),
        name="diff_prompt",
    )(scalars, q, k_stack, v_stack, g, cn)


def _diff_decode_kernel(pt_ref, sc_ref, q_ref, g_ref, cn_ref, kp_ref, vp_ref, kn_ref, vn_ref, o_ref,
                        vt_ref, bias_ref, slope_ref, m1_ref, l1_ref, a1_ref, m2_ref, l2_ref, a2_ref,
                        *, heads, page, n_pages, lam_init):
    p_idx = pl.program_id(1)
    past = n_pages * page
    dk = C_QK_DIM
    rq = q_ref.shape[0]
    lam = sc_ref[heads]
    stats = ((m1_ref, l1_ref, a1_ref), (m2_ref, l2_ref, a2_ref))

    def head_and_offset(nk):
        key = lax.broadcasted_iota(jnp.int32, (nk, rq), 0)
        qry = lax.broadcasted_iota(jnp.int32, (nk, rq), 1)
        same_head = (key & (heads - 1)) == (qry & (heads - 1))
        return same_head, _div_pow2(qry, heads) - _div_pow2(key, heads)

    @pl.when(p_idx == 0)
    def _():
        _flash_t_reset(stats)
        qh = lax.broadcasted_iota(jnp.int32, (1, rq), 1) & (heads - 1)
        slope = jnp.zeros((1, rq), F32)
        for hh in range(heads):
            slope = jnp.where(qh == hh, sc_ref[hh], slope)
        slope_ref[...] = slope * LOG2E
        same_head, off = head_and_offset(page * heads)
        bias_ref[...] = jnp.where(same_head, -(slope * LOG2E) * off.astype(F32), NEG)

    qs = (q_ref[:, :dk].astype(BF16), q_ref[:, dk:].astype(BF16))

    def attend(k_ref, v_ref, bias_t, shift):
        nk = k_ref.shape[0]
        for r0 in range(0, nk, LANES):
            vt_ref[:, r0:r0 + LANES] = v_ref[r0:r0 + LANES, :].T.astype(BF16)
        kb = k_ref[...].astype(BF16)
        _flash_t_tile((kb[:, :dk], kb[:, dk:]), vt_ref[:, 0:nk], qs, bias_t, shift, C_QK_DIM ** -0.5, stats)

    attend(kp_ref, vp_ref, bias_ref[...], slope_ref[...] * (past - p_idx * page).astype(F32))

    @pl.when(p_idx == n_pages - 1)
    def _():
        same_head, off = head_and_offset(kn_ref.shape[0])
        bias_new = jnp.where(same_head & (off >= 0), -slope_ref[...] * off.astype(F32), NEG)
        attend(kn_ref, vn_ref, bias_new, jnp.zeros((1, rq), F32))
        o_t = a1_ref[...] * (1.0 / l1_ref[...]) - lam * (a2_ref[...] * (1.0 / l2_ref[...]))
        o_t = o_t * lax.rsqrt(jnp.mean(o_t * o_t, axis=0, keepdims=True) + EPS)
        o_t = o_t * jnp.concatenate([cn_ref[...]] * (rq // LANES), axis=1) * (1.0 - lam_init)
        o_ref[...] = (o_t.T * _silu(g_ref[...])).astype(o_ref.dtype)


def _diff_decode(q, g, k_new, v_new, cache_k, cache_v, page_table, layer, scalars, c_norm, heads, lam_init):
    nb, rq, _ = q.shape
    dv = C_V_DIM
    n_odd, n_pool, page = cache_k.shape[:3]
    n_pages = page_table.shape[1]
    nk = page * heads
    assert rq % LANES == 0 and k_new.shape[1] <= nk
    ck = cache_k.reshape(n_odd, n_pool, nk, 2 * C_QK_DIM)
    cv = cache_v.reshape(n_odd, n_pool, nk, dv)
    cn = jnp.broadcast_to(c_norm[:, :, None], (c_norm.shape[0], dv, LANES))
    small_k = pl.BlockSpec((None, rq, 2 * C_QK_DIM), lambda b, p, pt: (b, 0, 0))
    small_v = pl.BlockSpec((None, rq, dv), lambda b, p, pt: (b, 0, 0))
    stat = [pltpu.VMEM((1, rq), F32), pltpu.VMEM((1, rq), F32), pltpu.VMEM((dv, rq), F32)]
    grid_spec = pltpu.PrefetchScalarGridSpec(
        num_scalar_prefetch=1,
        grid=(nb, n_pages),
        in_specs=[pl.BlockSpec(memory_space=pltpu.SMEM), small_k, small_v,
                  pl.BlockSpec((None, dv, LANES), lambda b, p, pt: (layer, 0, 0)),
                  pl.BlockSpec((None, None, nk, 2 * C_QK_DIM), lambda b, p, pt: (layer, pt[b, p], 0, 0)),
                  pl.BlockSpec((None, None, nk, dv), lambda b, p, pt: (layer, pt[b, p], 0, 0)),
                  small_k, small_v],
        out_specs=small_v,
        scratch_shapes=[pltpu.VMEM((dv, nk), BF16), pltpu.VMEM((nk, rq), F32), pltpu.VMEM((1, rq), F32)]
        + stat + stat)
    return pl.pallas_call(
        functools.partial(_diff_decode_kernel, heads=heads, page=page, n_pages=n_pages, lam_init=lam_init),
        out_shape=jax.ShapeDtypeStruct((nb, rq, dv), BF16),
        grid_spec=grid_spec,
        compiler_params=_params(("arbitrary", "arbitrary"), VMEM_MID),
        name="diff_decode",
    )(page_table, scalars, q, g, cn, ck, cv, k_new, v_new)


def _expand_exact(x, onehot3):
    hi = x.astype(BF16)
    r1 = x - hi.astype(F32)
    mid = r1.astype(BF16)
    lo = (r1 - mid.astype(F32)).astype(BF16)
    return jnp.dot(jnp.concatenate([hi, mid, lo], axis=1), onehot3, preferred_element_type=F32)


def _ssd_kernel(xs_ref, bm_ref, cm_ref, z_ref, dt_ref, wx_ref, wb_ref, wc_ref, bx_ref, bb_ref, bc_ref,
                dtb_ref, alog_ref, dskip_ref, bnorm_ref, *rest, rows, t_valid, n_heads, has_past):
    if has_past:
        cx_ref, cbm_ref, ccm_ref, h0_ref, y_ref, hout_ref, px_ref, pb_ref, pc_ref, h_ref, at_ref = rest
    else:
        y_ref, hout_ref, px_ref, pb_ref, pc_ref, h_ref, at_ref = rest
    g = pl.program_id(1)
    c = pl.program_id(2)
    ln = B_CHUNK
    hpg = n_heads // B_GROUPS
    gw = hpg * B_HEAD_DIM
    pad = 8

    @pl.when(c == 0)
    def _():
        if has_past:
            px_ref[0:pad, :] = cx_ref[...]
            pb_ref[0:pad, :] = cbm_ref[...]
            pc_ref[0:pad, :] = ccm_ref[...]
            h_ref[...] = h0_ref[...]
        else:
            px_ref[0:pad, :] = jnp.zeros((pad, gw), F32)
            pb_ref[0:pad, :] = jnp.zeros((pad, B_STATE), F32)
            pc_ref[0:pad, :] = jnp.zeros((pad, B_STATE), F32)
            h_ref[...] = jnp.zeros(h_ref.shape, F32)
        if rows < ln:
            for p_ref in (px_ref, pb_ref, pc_ref):
                p_ref[pad + rows:pad + ln, :] = jnp.zeros((ln - rows, p_ref.shape[1]), F32)

    def conv(raw_ref, p_ref, w_ref, b_ref):
        p_ref[pad:pad + rows, :] = raw_ref[...]
        out = b_ref[...] + p_ref[pad - 3:pad - 3 + ln, :] * w_ref[0:1, :]
        for j in range(1, B_CONV):
            out = out + p_ref[pad - 3 + j:pad - 3 + j + ln, :] * w_ref[j:j + 1, :]
        if rows == ln:
            p_ref[0:pad, :] = p_ref[ln:ln + pad, :]
        return _silu(out)

    xs = conv(xs_ref, px_ref, wx_ref, bx_ref)
    bm = conv(bm_ref, pb_ref, wb_ref, bb_ref)
    cm = conv(cm_ref, pc_ref, wc_ref, bc_ref)

    lane = lax.broadcasted_iota(jnp.int32, (ln, LANES), 1)
    rowi = lax.broadcasted_iota(jnp.int32, (ln, LANES), 0)
    if rows < ln:
        dt_raw = jnp.concatenate([dt_ref[...], jnp.zeros((ln - rows, LANES), F32)], axis=0)
    else:
        dt_raw = dt_ref[...]
    live = (lane < n_heads) & (rowi < t_valid)
    dt = jnp.where(live, jax.nn.softplus(jnp.where(live, dt_raw, 0.0) + dtb_ref[...]), 0.0)
    a_step = dt * (-jnp.exp(alog_ref[...]))
    tri = jnp.where(lax.broadcasted_iota(jnp.int32, (ln, ln), 0) >= lax.broadcasted_iota(jnp.int32, (ln, ln), 1),
                    1.0, 0.0)
    acum = jnp.dot(tri, a_step, preferred_element_type=F32, precision=lax.Precision.HIGHEST)
    at_ref[...] = acum.T

    head_of_lane = g * hpg + _div_pow2(lax.broadcasted_iota(jnp.int32, (LANES, gw), 1), B_HEAD_DIM)
    oh_ch = jnp.where(lax.broadcasted_iota(jnp.int32, (LANES, gw), 0) == head_of_lane, 1.0, 0.0).astype(BF16)
    oh_ch3 = jnp.concatenate([oh_ch, oh_ch, oh_ch], axis=0)
    head_of_blk = g * hpg + _div_pow2(lax.broadcasted_iota(jnp.int32, (LANES, hpg * ln), 1), ln)
    oh_blk = jnp.where(lax.broadcasted_iota(jnp.int32, (LANES, hpg * ln), 0) == head_of_blk, 1.0, 0.0).astype(BF16)
    oh_blk3 = jnp.concatenate([oh_blk, oh_blk, oh_blk], axis=0)
    dt_e = _expand_exact(dt, oh_ch3)
    ac_e = _expand_exact(acum, oh_ch3)
    ac_col = _expand_exact(acum, oh_blk3)
    last_e = ac_e[ln - 1:ln, :]

    xdt = xs * dt_e
    xdt_b = xdt.astype(BF16)
    w_end = (xdt * jnp.exp(last_e - ac_e)).astype(BF16)
    bm_b = bm.astype(BF16)
    cm_b = cm.astype(BF16)
    cb = lax.dot_general(cm_b, bm_b, (((1,), (1,)), ((), ())), preferred_element_type=F32)
    h_prev = h_ref[...]
    y = jnp.dot(cm_b, h_prev.astype(BF16), preferred_element_type=F32) * jnp.exp(ac_e)

    causal = lax.broadcasted_iota(jnp.int32, (ln, ln), 0) >= lax.broadcasted_iota(jnp.int32, (ln, ln), 1)
    lane_pair = lax.broadcasted_iota(jnp.int32, (ln, 2 * B_HEAD_DIM), 1)
    y_parts = []
    for pr in range(hpg // 2):
        ms = []
        for r in (2 * pr, 2 * pr + 1):
            seg = ac_col[:, r * ln:(r + 1) * ln] - at_ref[pl.ds(g * hpg + r, 1), :]
            decay = jnp.exp(jnp.where(causal, seg, -jnp.inf))
            ms.append((cb * decay).astype(BF16))
        xp = xdt_b[:, pr * 2 * B_HEAD_DIM:(pr + 1) * 2 * B_HEAD_DIM]
        zero = jnp.zeros_like(xp)
        rhs = jnp.concatenate([jnp.where(lane_pair < B_HEAD_DIM, xp, zero),
                               jnp.where(lane_pair >= B_HEAD_DIM, xp, zero)], axis=0)
        y_parts.append(jnp.dot(jnp.concatenate(ms, axis=1), rhs, preferred_element_type=F32))
    y = y + jnp.concatenate(y_parts, axis=1)

    upd = jnp.dot(bm.T.astype(BF16), w_end, preferred_element_type=F32)
    h_new = h_prev * jnp.exp(last_e) + upd
    h_ref[...] = h_new
    hout_ref[...] = h_new

    y = (y + dskip_ref[...] * xs)
    if rows < ln:
        y = y[0:rows, :]
    y = y * _silu(z_ref[...])
    y = y * lax.rsqrt(jnp.mean(y * y, axis=-1, keepdims=True) + EPS) * bnorm_ref[...]
    y_ref[...] = y.astype(y_ref.dtype)


def _ssd(proj, nb, t_rows, t_valid, xc, bc, cc, zc, dtc, layer, conv_w, conv_b, dt_bias_p, a_log_p,
         d_skip_e, b_norm, past=None):
    ln = B_CHUNK
    rows = min(ln, t_rows)
    nc = t_rows // rows
    width = d_skip_e.shape[1]
    n_heads = width // B_HEAD_DIM
    gw = width // B_GROUPS
    st = B_STATE
    xoff = 0
    boff = width // st
    coff = boff + B_GROUPS
    ne = conv_w.shape[0]
    cbias = conv_b.reshape(ne, 1, conv_b.shape[1])
    row_blk = lambda b, g, c: b * nc + c
    in_specs = [
        pl.BlockSpec((rows, gw), lambda b, g, c: (row_blk(b, g, c), xc + g)),
        pl.BlockSpec((rows, st), lambda b, g, c: (row_blk(b, g, c), bc + g)),
        pl.BlockSpec((rows, st), lambda b, g, c: (row_blk(b, g, c), cc + g)),
        pl.BlockSpec((rows, gw), lambda b, g, c: (row_blk(b, g, c), zc + g)),
        pl.BlockSpec((rows, LANES), lambda b, g, c: (row_blk(b, g, c), dtc)),
        pl.BlockSpec((None, B_CONV, gw), lambda b, g, c: (layer, 0, xoff + g)),
        pl.BlockSpec((None, B_CONV, st), lambda b, g, c: (layer, 0, boff + g)),
        pl.BlockSpec((None, B_CONV, st), lambda b, g, c: (layer, 0, coff + g)),
        pl.BlockSpec((None, 1, gw), lambda b, g, c: (layer, 0, xoff + g)),
        pl.BlockSpec((None, 1, st), lambda b, g, c: (layer, 0, boff + g)),
        pl.BlockSpec((None, 1, st), lambda b, g, c: (layer, 0, coff + g)),
        pl.BlockSpec((None, 1, LANES), lambda b, g, c: (layer, 0, 0)),
        pl.BlockSpec((None, 1, LANES), lambda b, g, c: (layer, 0, 0)),
        pl.BlockSpec((None, 1, gw), lambda b, g, c: (layer, 0, g)),
        pl.BlockSpec((None, 1, gw), lambda b, g, c: (layer, 0, g)),
    ]
    args = [proj, proj, proj, proj, proj, conv_w, conv_w, conv_w, cbias, cbias, cbias,
            dt_bias_p, a_log_p, d_skip_e.reshape(ne, 1, width), b_norm.reshape(ne, 1, width)]
    if past is not None:
        conv_hist, h0t = past
        in_specs += [
            pl.BlockSpec((None, 8, gw), lambda b, g, c: (b, 0, xoff + g)),
            pl.BlockSpec((None, 8, st), lambda b, g, c: (b, 0, boff + g)),
            pl.BlockSpec((None, 8, st), lambda b, g, c: (b, 0, coff + g)),
            pl.BlockSpec((None, st, gw), lambda b, g, c: (b, 0, g)),
        ]
        args += [conv_hist, conv_hist, conv_hist, h0t]
    y, h_t = pl.pallas_call(
        functools.partial(_ssd_kernel, rows=rows, t_valid=t_valid, n_heads=n_heads, has_past=past is not None),
        out_shape=[jax.ShapeDtypeStruct((nb * t_rows, width), BF16),
                   jax.ShapeDtypeStruct((nb, st, width), F32)],
        grid=(nb, B_GROUPS, nc),
        in_specs=in_specs,
        out_specs=[pl.BlockSpec((rows, gw), lambda b, g, c: (row_blk(b, g, c), g)),
                   pl.BlockSpec((None, st, gw), lambda b, g, c: (b, 0, g))],
        scratch_shapes=[pltpu.VMEM((ln + 8, gw), F32), pltpu.VMEM((ln + 8, st), F32),
                        pltpu.VMEM((ln + 8, st), F32), pltpu.VMEM((st, gw), F32),
                        pltpu.VMEM((LANES, ln), F32)],
        compiler_params=_params(("arbitrary", "arbitrary", "arbitrary"), VMEM_MID),
        name="ssd",
    )(*args)
    return y, h_t


def _shift_kernel(a_ref, nxt_ref, new_ref, o_ref, *, blk, n_blocks):
    g = pl.program_id(1)
    o_ref[0:blk - 1] = a_ref[1:blk]

    @pl.when(g < n_blocks - 1)
    def _():
        o_ref[blk - 1] = nxt_ref[0]

    @pl.when(g == n_blocks - 1)
    def _():
        o_ref[blk - 1] = new_ref[0]


def _shift_window(cache, new_rows, blk=64):
    e, b, past, heads, hd = cache.shape
    dec = new_rows.shape[2]
    assert past % (dec * blk) == 0
    grp = dec * heads
    n_grp = past // dec
    n_blocks = n_grp // blk
    c4 = cache.reshape(e * b, n_grp, grp, hd)
    n4 = new_rows.reshape(e * b, 1, grp, hd)
    out = pl.pallas_call(
        functools.partial(_shift_kernel, blk=blk, n_blocks=n_blocks),
        out_shape=jax.ShapeDtypeStruct(c4.shape, cache.dtype),
        grid=(e * b, n_blocks),
        in_specs=[pl.BlockSpec((None, blk, grp, hd), lambda n, g: (n, g, 0, 0)),
                  pl.BlockSpec((None, 1, grp, hd), lambda n, g: (n, jnp.minimum((g + 1) * blk, n_grp - 1), 0, 0)),
                  pl.BlockSpec((None, 1, grp, hd), lambda n, g: (n, 0, 0, 0))],
        out_specs=pl.BlockSpec((None, blk, grp, hd), lambda n, g: (n, g, 0, 0)),
        compiler_params=_params(("arbitrary", "arbitrary"), VMEM_MID),
        name="window_shift",
    )(c4, c4, n4)
    return out.reshape(cache.shape)


def _alibi_slopes(n_heads):
    return jnp.exp2(-8.0 * jnp.arange(1, n_heads + 1, dtype=F32) / n_heads)


def _pad_lanes(v):
    return jnp.pad(v, ((0, 0), (0, LANES - v.shape[1]))).reshape(v.shape[0], 1, LANES)


def kernel(x_prompt, x_sample, cache_win_k, cache_win_v, state_conv, state_ssm, cache_diff_k, cache_diff_v,
           page_table, c_prompt, c_sample, ln_g, ln_b, w_mod, b_mod, w_in_even, w_out_even, conv_w, conv_b,
           dt_bias, a_log, d_skip, b_norm, w_in_odd, w_out_odd, lam_q1, lam_k1, lam_q2, lam_k2, c_norm):
    nbp, seq, d = x_prompt.shape
    nbs, dec_seq, _ = x_sample.shape
    depth = w_mod.shape[0]
    alpha = (2 * depth) ** 0.25
    a_heads = cache_win_k.shape[3]
    a_width = a_heads * A_HEAD_DIM
    c_heads = cache_diff_k.shape[3]
    b_width = b_norm.shape[1]
    conv_dim = conv_w.shape[2]
    b_heads = dt_bias.shape[1]
    assert dec_seq <= DEC_PAD and seq % B_CHUNK == 0

    slopes_a = _alibi_slopes(a_heads)
    slopes_c = _alibi_slopes(c_heads)

    c_rows = jnp.concatenate([c_prompt, c_sample], axis=0)
    c_rows = jnp.pad(c_rows, ((0, (-c_rows.shape[0]) % 8), (0, 0)))
    mods = _mods(c_rows, w_mod, b_mod)
    mods_p = mods[:, :nbp].reshape(depth, nbp, 1, 3 * d)
    mods_s = jnp.repeat(mods[:, nbp:nbp + nbs], DEC_PAD, axis=1).reshape(depth, 1, nbs * DEC_PAD, 3 * d)

    xp = x_prompt.reshape(nbp * seq, d)
    xs = jnp.pad(x_sample, ((0, 0), (0, DEC_PAD - dec_seq), (0, 0))).reshape(nbs * DEC_PAD, d)
    ms_rows = nbs * DEC_PAD
    tm_row = 256

    hp = _modulate(xp, mods_p, 0, tm_row, seq)
    hs = _modulate(xs, mods_s, 0, ms_rows, ms_rows)

    dt_bias_p, a_log_p = _pad_lanes(dt_bias), _pad_lanes(a_log)
    d_skip_e = jnp.repeat(d_skip, B_HEAD_DIM, axis=1)
    conv_hist = jnp.pad(state_conv, ((0, 0), (0, 0), (8 - (B_CONV - 1), 0), (0, 0)))
    h0t = state_ssm.transpose(0, 1, 4, 2, 3).reshape(state_ssm.shape[0], nbs, B_STATE, b_width)

    def rows_th(a, width):
        return a.reshape(nbs, DEC_PAD * (a.shape[1] // width), width)

    n_even, n_odd = w_in_even.shape[0], w_in_odd.shape[0]
    past_win = cache_win_k.shape[2]
    assert seq <= A_PATTERNS[-1][0] and past_win == A_PATTERNS[-1][0]
    w_in_even_t = jnp.swapaxes(w_in_even, 1, 2)
    wk_p = wv_p = dk_p = dv_p = None
    outs = {k: [] for k in ("cv_p", "ss_p", "wk_s", "wv_s", "cv_s", "ss_s", "dk_s", "dv_s")}
    for l in range(depth):
        last = l == depth - 1
        if l % 2 == 0:
            e = l // 2
            gw = b_width // B_GROUPS
            r0 = 3 * a_width
            z0, x0 = a_width, a_width + b_width
            ssd_cols = dict(xc=x0 // gw, bc=(x0 + b_width) // LANES, cc=(x0 + b_width) // LANES + B_GROUPS,
                            zc=z0 // gw, dtc=(x0 + conv_dim) // LANES)
            ssd_w = dict(layer=e, conv_w=conv_w, conv_b=conv_b, dt_bias_p=dt_bias_p, a_log_p=a_log_p,
                         d_skip_e=d_skip_e, b_norm=b_norm)
            proj = functools.partial(_mm, [hp], w_in_even_t, e, 1024, extra=[hs], w_t=True)
            aq_p, aq_s = proj(0, a_width)
            wk_p, ak_s = proj(a_width, a_width, stack=(wk_p, n_even))
            wv_p, av_s = proj(2 * a_width, a_width, stack=(wv_p, n_even))
            rest_p, rest_s = proj(r0)
            ya_p = _dil_prompt(aq_p, wk_p, wv_p, rest_p, e, slopes_a, nbp, seq, a_heads, 0)
            yb_p, ht_p = _ssd(rest_p, nbp, seq, seq, **ssd_cols, **ssd_w)
            outs["cv_p"].append(rest_p.reshape(nbp, seq, -1)[:, seq - (B_CONV - 1):, x0:x0 + conv_dim])
            outs["ss_p"].append(ht_p.reshape(nbp, B_STATE, b_heads, B_HEAD_DIM).transpose(0, 2, 3, 1))
            ya_s = _dil_decode(rows_th(aq_s, A_HEAD_DIM), rows_th(rest_s[:, :a_width], A_HEAD_DIM),
                               rows_th(ak_s, A_HEAD_DIM), rows_th(av_s, A_HEAD_DIM),
                               cache_win_k, cache_win_v, e, slopes_a, a_heads)
            ya_s = ya_s.reshape(ms_rows, a_width)
            yb_s, ht_s = _ssd(rest_s, nbs, DEC_PAD, dec_seq, past=(conv_hist[e], h0t[e]), **ssd_cols, **ssd_w)
            new_rows = lambda a: a.reshape(nbs, DEC_PAD, a_heads, A_HEAD_DIM)[:, :dec_seq]
            outs["wk_s"].append(new_rows(ak_s))
            outs["wv_s"].append(new_rows(av_s))
            xbc_new = rest_s.reshape(nbs, DEC_PAD, -1)[:, :dec_seq, x0:x0 + conv_dim]
            outs["cv_s"].append(jnp.concatenate([state_conv[e], xbc_new], axis=1)[:, -(B_CONV - 1):])
            outs["ss_s"].append(ht_s.reshape(nbs, B_STATE, b_heads, B_HEAD_DIM).transpose(0, 2, 3, 1))
            yp, ys = _mm([ya_p, yb_p], w_out_even, e, 512, extra=[ya_s, yb_s])
        else:
            o = l // 2
            lam_init = 0.8 - 0.6 * math.exp(-0.3 * l)
            lam = (jnp.exp(jnp.sum(lam_q1[o] * lam_k1[o])) - jnp.exp(jnp.sum(lam_q2[o] * lam_k2[o])) + lam_init)
            scalars = jnp.concatenate([slopes_c, lam.reshape(1)])
            qkw = c_heads * 2 * C_QK_DIM
            vw = c_heads * C_V_DIM
            proj = functools.partial(_mm, [hp], w_in_odd, o, 1024, extra=[hs])
            q_p, q_s = proj(0, qkw)
            dk_p, k_s = proj(qkw, qkw, stack=(dk_p, n_odd))
            dv_p, v_s = proj(2 * qkw, vw, stack=(dv_p, n_odd))
            g_p, g_s = proj(2 * qkw + vw, vw)
            y_p = _diff_prompt(q_p, dk_p, dv_p, g_p, o, scalars, c_norm, nbp, seq, c_heads, lam_init)
            y_s = _diff_decode(rows_th(q_s, 2 * C_QK_DIM), rows_th(g_s, C_V_DIM), rows_th(k_s, 2 * C_QK_DIM),
                               rows_th(v_s, C_V_DIM), cache_diff_k, cache_diff_v, page_table, o, scalars, c_norm,
                               c_heads, lam_init)
            outs["dk_s"].append(k_s.reshape(nbs, DEC_PAD, c_heads, 2 * C_QK_DIM)[:, :dec_seq])
            outs["dv_s"].append(v_s.reshape(nbs, DEC_PAD, c_heads, C_V_DIM)[:, :dec_seq])
            yp, ys = _mm([y_p], w_out_odd, o, 1024, extra=[y_s.reshape(ms_rows, vw)])
        xp, hp = _post(xp, yp, mods_p, l, ln_g, ln_b, alpha, tm_row, seq, not last)
        xs, hs = _post(xs, ys, mods_s, l, ln_g, ln_b, alpha, ms_rows, ms_rows, not last)

    y_prompt = xp.reshape(nbp, seq, d)
    y_sample = xs.reshape(nbs, DEC_PAD, d)[:, :dec_seq]
    st = lambda k: jnp.stack(outs[k])
    wk_s = _shift_window(cache_win_k, st("wk_s"))
    wv_s = _shift_window(cache_win_v, st("wv_s"))
    return (y_prompt, y_sample,
            wk_p.reshape(n_even, nbp, seq, a_heads, A_HEAD_DIM), wv_p.reshape(n_even, nbp, seq, a_heads, A_HEAD_DIM),
            st("cv_p"), st("ss_p"),
            dk_p.reshape(n_odd, nbp, seq, c_heads, 2 * C_QK_DIM), dv_p.reshape(n_odd, nbp, seq, c_heads, C_V_DIM),
            wk_s, wv_s, st("cv_s"), st("ss_s"), st("dk_s"), st("dv_s"))
```

```python
import functools
import math

import jax
import jax.numpy as jnp
from jax import lax
from jax.experimental import pallas as pl
from jax.experimental.pallas import tpu as pltpu

F32 = jnp.float32
BF16 = jnp.bfloat16
EPS = 1e-5
NEG = -1e30
A_PATTERNS = ((128, 1), (512, 4), (2048, 16))
A_HEAD_DIM = 128
B_HEAD_DIM = 64
B_GROUPS = 8
B_STATE = 128
B_CONV = 4
B_CHUNK = 128
C_QK_DIM = 128
C_V_DIM = 256
DEC_PAD = 8
LANES = 128
VMEM_BIG = 48 * 1024 * 1024
VMEM_MID = 32 * 1024 * 1024


def _params(sem, vmem=None):
    return pltpu.CompilerParams(dimension_semantics=sem, vmem_limit_bytes=vmem)


def _silu(x):
    return x * jax.nn.sigmoid(x)


def _div_pow2(x, n):
    assert n & (n - 1) == 0
    return x >> (n.bit_length() - 1)


def _mods_kernel(c_ref, w_ref, b_ref, o_ref, *, kc):
    c = c_ref[...]
    s = _silu(c).astype(BF16)
    d = c.shape[1]
    acc = jnp.zeros(o_ref.shape, F32)
    for k0 in range(0, d, kc):
        acc = acc + jnp.dot(s[:, k0:k0 + kc], w_ref[k0:k0 + kc, :].astype(BF16),
                            preferred_element_type=F32)
    o_ref[...] = acc + b_ref[...]


def _mods(c_rows, w_mod, b_mod, tn=512):
    nl, d, n = w_mod.shape
    r = c_rows.shape[0]
    return pl.pallas_call(
        functools.partial(_mods_kernel, kc=512),
        out_shape=jax.ShapeDtypeStruct((nl, r, n), F32),
        grid=(nl, n // tn),
        in_specs=[pl.BlockSpec((r, d), lambda l, j: (0, 0)),
                  pl.BlockSpec((None, d, tn), lambda l, j: (l, 0, j)),
                  pl.BlockSpec((None, 1, tn), lambda l, j: (l, 0, j))],
        out_specs=pl.BlockSpec((None, r, tn), lambda l, j: (l, 0, j)),
        compiler_params=_params(("arbitrary", "arbitrary"), VMEM_MID),
        name="mods",
    )(c_rows, w_mod, b_mod.reshape(nl, 1, n))


def _mm_kernel(*refs, k_sizes, n_extra, cast_rows, w_t):
    nx = len(k_sizes)
    x_refs, e_refs, w_ref = refs[:nx], refs[nx:nx + n_extra], refs[nx + n_extra]
    wb_ref = refs[-1]
    o_ref = refs[-3] if n_extra else refs[-2]

    def product(part_refs):
        acc = None
        off = 0
        for x_ref, ks in zip(part_refs, k_sizes):
            if w_t:
                part = lax.dot_general(x_ref[...], wb_ref[:, off:off + ks], (((1,), (1,)), ((), ())),
                                       preferred_element_type=F32)
            else:
                part = jnp.dot(x_ref[...], wb_ref[off:off + ks, :], preferred_element_type=F32)
            acc = part if acc is None else acc + part
            off += ks
        return acc

    @pl.when(pl.program_id(1) == 0)
    def _():
        def body(r, carry):
            rows = pl.ds(pl.multiple_of(r * cast_rows, cast_rows), cast_rows)
            wb_ref[rows, :] = w_ref[rows, :].astype(BF16)
            return carry
        lax.fori_loop(0, wb_ref.shape[0] // cast_rows, body, 0)
        if n_extra:
            refs[-2][...] = product(e_refs)

    o_ref[...] = product(x_refs)


def _mm(x_parts, w, layer, tm, col0=0, ncols=None, stack=None, extra=None, w_t=False, tn=512):
    m = x_parts[0].shape[0]
    k_sizes = tuple(x.shape[1] for x in x_parts)
    k = sum(k_sizes)
    n = w.shape[1] if w_t else w.shape[2]
    ncols = n - col0 if ncols is None else ncols
    cast_rows = 64 if w_t else 256
    assert (w.shape[2] if w_t else w.shape[1]) == k and m % tm == 0 and k % 256 == 0 and col0 % tn == 0
    cb0 = col0 // tn
    in_specs = [pl.BlockSpec((tm, ks), lambda j, i: (i, 0)) for ks in k_sizes]
    args = list(x_parts)
    out_shape, out_specs = [], []
    if extra is not None:
        me = extra[0].shape[0]
        assert tuple(x.shape[1] for x in extra) == k_sizes
        in_specs += [pl.BlockSpec((me, ks), lambda j, i: (0, 0)) for ks in k_sizes]
        args += list(extra)
    if w_t:
        in_specs.append(pl.BlockSpec((None, tn, k), lambda j, i: (layer, cb0 + j, 0)))
    else:
        in_specs.append(pl.BlockSpec((None, k, tn), lambda j, i: (layer, 0, cb0 + j)))
    args.append(w)
    aliases = {}
    if stack is None:
        out_shape.append(jax.ShapeDtypeStruct((m, ncols), F32))
        out_specs.append(pl.BlockSpec((tm, tn), lambda j, i: (i, j)))
    else:
        buf, depth = stack
        assert ncols % tn == 0
        out_shape.append(jax.ShapeDtypeStruct((depth, m, ncols), F32))
        out_specs.append(pl.BlockSpec((None, tm, tn), lambda j, i: (layer, i, j)))
        if buf is not None:
            in_specs.append(pl.BlockSpec(memory_space=pl.ANY))
            args.append(buf)
            aliases = {len(args) - 1: 0}
    if extra is not None:
        out_shape.append(jax.ShapeDtypeStruct((me, ncols), F32))
        out_specs.append(pl.BlockSpec((me, tn), lambda j, i: (0, j)))
    res = pl.pallas_call(
        functools.partial(_mm_kernel, k_sizes=k_sizes, n_extra=0 if extra is None else len(extra),
                          cast_rows=cast_rows, w_t=w_t),
        out_shape=out_shape,
        grid=(pl.cdiv(ncols, tn), m // tm),
        in_specs=in_specs,
        out_specs=out_specs,
        scratch_shapes=[pltpu.VMEM((tn, k) if w_t else (k, tn), BF16)],
        input_output_aliases=aliases,
        compiler_params=_params(("arbitrary", "arbitrary"), VMEM_BIG),
        name="proj_mm",
    )(*args)
    return (res[0], res[1]) if extra is not None else res[0]


def _modulate_kernel(x_ref, sh_ref, sc_ref, h_ref):
    h_ref[...] = (x_ref[...] * (1.0 + sc_ref[...]) + sh_ref[...]).astype(h_ref.dtype)


def _post_kernel(x_ref, y_ref, gt_ref, g_ref, b_ref, *rest, alpha, with_next):
    if with_next:
        sh_ref, sc_ref, xo_ref, h_ref = rest
    else:
        (xo_ref,) = rest
    r = alpha * x_ref[...] + (1.0 + gt_ref[...]) * y_ref[...]
    mu = jnp.mean(r, axis=-1, keepdims=True)
    rc = r - mu
    var = jnp.mean(rc * rc, axis=-1, keepdims=True)
    xn = rc * lax.rsqrt(var + EPS) * g_ref[...] + b_ref[...]
    xo_ref[...] = xn
    if with_next:
        h_ref[...] = (xn * (1.0 + sc_ref[...]) + sh_ref[...]).astype(h_ref.dtype)


def _mod_spec(mods, layer, part, tm, rows_per_seq):
    d = mods.shape[3] // 3
    r = mods.shape[2]
    if r == 1:
        tiles = rows_per_seq // tm
        return pl.BlockSpec((None, None, 1, d), lambda i: (layer, i // tiles, 0, part))
    return pl.BlockSpec((None, None, r, d), lambda i: (layer, 0, i, part))


def _modulate(x, mods, layer, tm, rows_per_seq):
    m, d = x.shape
    row = pl.BlockSpec((tm, d), lambda i: (i, 0))
    return pl.pallas_call(
        _modulate_kernel,
        out_shape=jax.ShapeDtypeStruct((m, d), BF16),
        grid=(m // tm,),
        in_specs=[row, _mod_spec(mods, layer, 0, tm, rows_per_seq),
                  _mod_spec(mods, layer, 1, tm, rows_per_seq)],
        out_specs=row,
        compiler_params=_params(("arbitrary",), VMEM_MID),
        name="modulate",
    )(x, mods, mods)


def _post(x, y, mods, layer, ln_g, ln_b, alpha, tm, rows_per_seq, with_next):
    m, d = x.shape
    nl = ln_g.shape[0]
    row = pl.BlockSpec((tm, d), lambda i: (i, 0))
    vec = pl.BlockSpec((None, 1, d), lambda i: (layer, 0, 0))
    in_specs = [row, row, _mod_spec(mods, layer, 2, tm, rows_per_seq), vec, vec]
    args = [x, y, mods, ln_g.reshape(nl, 1, d), ln_b.reshape(nl, 1, d)]
    out_shape = [jax.ShapeDtypeStruct((m, d), F32)]
    out_specs = [row]
    if with_next:
        in_specs += [_mod_spec(mods, layer + 1, 0, tm, rows_per_seq),
                     _mod_spec(mods, layer + 1, 1, tm, rows_per_seq)]
        args += [mods, mods]
        out_shape.append(jax.ShapeDtypeStruct((m, d), BF16))
        out_specs.append(row)
    res = pl.pallas_call(
        functools.partial(_post_kernel, alpha=alpha, with_next=with_next),
        out_shape=out_shape,
        grid=(m // tm,),
        in_specs=in_specs,
        out_specs=out_specs,
        compiler_params=_params(("arbitrary",), VMEM_MID),
        name="post_ln",
    )(*args)
    return (res[0], res[1]) if with_next else (res[0], None)


def _pattern_count(dist):
    cnt = jnp.zeros(dist.shape, F32)
    for window, dil in A_PATTERNS:
        ok = (dist >= 0) & (dist <= window) & ((dist & (dil - 1)) == 0)
        cnt = cnt + jnp.where(ok, 1.0, 0.0)
    return cnt


M_FLOOR = -1e20
LOG2E = math.log2(math.e)


def _flash_t_tile(k_maps, v_t, q_maps, bias_t, cj, scale, stats):
    ps, alphas = [], []
    for k, q, (m_ref, l_ref, _) in zip(k_maps, q_maps, stats):
        s = lax.dot_general(k, q, (((1,), (1,)), ((), ())), preferred_element_type=F32) * (scale * LOG2E) + bias_t
        m_prev = m_ref[...]
        m_new = jnp.maximum(m_prev, jnp.max(s, axis=0, keepdims=True) - cj)
        p = jnp.exp2(s - (m_new + cj))
        alpha = jnp.exp2(m_prev - m_new)
        l_ref[...] = alpha * l_ref[...] + jnp.sum(p, axis=0, keepdims=True)
        m_ref[...] = m_new
        ps.append(p.astype(BF16))
        alphas.append(alpha)
    pv = jnp.dot(v_t, ps[0] if len(ps) == 1 else jnp.concatenate(ps, axis=1), preferred_element_type=F32)
    tq = ps[0].shape[1]
    for n, (alpha, (_, _, acc_ref)) in enumerate(zip(alphas, stats)):
        acc_ref[...] = alpha * acc_ref[...] + pv[:, n * tq:(n + 1) * tq]


def _flash_t_diag(k_of, vt_ref, row0, q_of, bias_diag, scale, stats):
    tile = bias_diag.shape[0]
    half = tile // 2
    early = pl.ds(row0, half)
    late = pl.ds(row0 + half, half)
    _flash_t_tile(k_of(early), vt_ref[:, early], q_of(0), bias_diag[0:half, :], 0.0, scale, stats)
    hi = tuple(tuple(r.at[:, half:] for r in st) for st in stats)
    _flash_t_tile(k_of(late), vt_ref[:, late], q_of(half), bias_diag[half:, half:], 0.0, scale, hi)


def _flash_t_reset(stats):
    for m_ref, l_ref, acc_ref in stats:
        m_ref[...] = jnp.full(m_ref.shape, M_FLOOR, F32)
        l_ref[...] = jnp.zeros(l_ref.shape, F32)
        acc_ref[...] = jnp.zeros(acc_ref.shape, F32)


def _stage_kv(k_ref, v_ref, kb_ref, vt_ref):
    t = k_ref.shape[0]
    for r0 in range(0, t, 256):
        kb_ref[r0:r0 + 256, :] = k_ref[r0:r0 + 256, :].astype(BF16)
    for r0 in range(0, t, LANES):
        vt_ref[:, r0:r0 + LANES] = v_ref[r0:r0 + LANES, :].T.astype(BF16)


def _dil_prompt_kernel(slopes_ref, q_ref, k_ref, v_ref, g_ref, logc_ref, o_ref,
                       kb_ref, vt_ref, bias_ref, m_ref, l_ref, acc_ref, *, tile, window):
    h = pl.program_id(1)
    i = pl.program_id(2)
    slope = slopes_ref[h]
    n_off = bias_ref.shape[0]

    @pl.when(i == 0)
    def _():
        _stage_kv(k_ref, v_ref, kb_ref, vt_ref)
        rel_t = (lax.broadcasted_iota(jnp.int32, (tile, tile), 1)
                 - lax.broadcasted_iota(jnp.int32, (tile, tile), 0)).astype(F32)
        for d in range(n_off):
            bias_ref[d] = (logc_ref[d] - slope * rel_t) * LOG2E

    stats = ((m_ref, l_ref, acc_ref),)
    _flash_t_reset(stats)
    scale = A_HEAD_DIM ** -0.5
    q = q_ref[...].astype(BF16)

    def body(j, carry):
        rows = pl.ds(pl.multiple_of(j * tile, tile), tile)
        cj = (slope * LOG2E) * ((i - j) * tile).astype(F32)
        _flash_t_tile((kb_ref[rows, :],), vt_ref[:, rows], (q,), bias_ref[i - j], cj, scale, stats)
        return carry

    lax.fori_loop(jnp.maximum(i - (n_off - 1), 0), i, body, 0)
    _flash_t_diag(lambda rows: (kb_ref[rows, :],), vt_ref, pl.multiple_of(i * tile, tile),
                  lambda lo: (q[lo:, :],), bias_ref.at[0], scale, stats)
    o = (acc_ref[...] * (1.0 / l_ref[...])).T
    o_ref[...] = (o * _silu(g_ref[...])).astype(o_ref.dtype)


def _log_count_tiles(tile, n_off):
    import numpy as np
    dist = (np.arange(n_off)[:, None, None] * tile + np.arange(tile)[None, None, :] - np.arange(tile)[None, :, None])
    cnt = np.zeros(dist.shape, np.float64)
    for window, dil in A_PATTERNS:
        cnt += (dist >= 0) & (dist <= window) & (dist % dil == 0)
    return jnp.asarray(np.where(cnt > 0, np.log(np.maximum(cnt, 1.0)), NEG), F32)


def _dil_prompt(q, k_stack, v_stack, g, layer, slopes, nb, t, heads, gc, tile=512):
    hd = A_HEAD_DIM
    nq = t // tile
    window = A_PATTERNS[-1][0]
    n_off = min(nq, window // tile + 2)
    logc = _log_count_tiles(tile, n_off)
    return pl.pallas_call(
        functools.partial(_dil_prompt_kernel, tile=tile, window=window),
        out_shape=jax.ShapeDtypeStruct((nb * t, heads * hd), BF16),
        grid=(nb, heads, nq),
        in_specs=[pl.BlockSpec(memory_space=pltpu.SMEM),
                  pl.BlockSpec((tile, hd), lambda b, h, i: (b * nq + i, h)),
                  pl.BlockSpec((None, t, hd), lambda b, h, i: (layer, b, h)),
                  pl.BlockSpec((None, t, hd), lambda b, h, i: (layer, b, h)),
                  pl.BlockSpec((tile, hd), lambda b, h, i: (b * nq + i, gc + h)),
                  pl.BlockSpec((n_off, tile, tile), lambda b, h, i: (0, 0, 0))],
        out_specs=pl.BlockSpec((tile, hd), lambda b, h, i: (b * nq + i, h)),
        scratch_shapes=[pltpu.VMEM((t, hd), BF16), pltpu.VMEM((hd, t), BF16),
                        pltpu.VMEM((n_off, tile, tile), F32),
                        pltpu.VMEM((1, tile), F32), pltpu.VMEM((1, tile), F32), pltpu.VMEM((hd, tile), F32)],
        compiler_params=_params(("arbitrary", "arbitrary", "arbitrary"), VMEM_BIG),
        name="dilated_prompt",
    )(slopes, q, k_stack, v_stack, g, logc)


def _dil_decode_kernel(slopes_ref, q_ref, g_ref, kc_ref, vc_ref, kn_ref, vn_ref, tab_ref, tabn_ref, o_ref,
                       vt_ref, base_ref, slope_ref, m_ref, l_ref, acc_ref, *, heads, past, kb):
    c = pl.program_id(1)
    nc = past * heads // kb
    rq = q_ref.shape[0]
    stats = ((m_ref, l_ref, acc_ref),)

    def token_offset(nk):
        key = lax.broadcasted_iota(jnp.int32, (nk, rq), 0)
        qry = lax.broadcasted_iota(jnp.int32, (nk, rq), 1)
        return (_div_pow2(qry, heads) - _div_pow2(key, heads)).astype(F32)

    @pl.when(c == 0)
    def _():
        _flash_t_reset(stats)
        qh = lax.broadcasted_iota(jnp.int32, (1, rq), 1) & (heads - 1)
        slope = jnp.zeros((1, rq), F32)
        for hh in range(heads):
            slope = jnp.where(qh == hh, slopes_ref[hh], slope)
        slope_ref[...] = slope * LOG2E
        base_ref[...] = -(slope * LOG2E) * token_offset(kb)

    q = (q_ref[...].astype(BF16),)

    def attend(k_ref, v_ref, bias_t, shift):
        nk = k_ref.shape[0]
        for r0 in range(0, nk, LANES):
            vt_ref[:, r0:r0 + LANES] = v_ref[r0:r0 + LANES, :].T.astype(BF16)
        _flash_t_tile((k_ref[...].astype(BF16),), vt_ref[:, 0:nk], q, bias_t, shift, A_HEAD_DIM ** -0.5, stats)

    attend(kc_ref, vc_ref, tab_ref[...] + base_ref[...],
           slope_ref[...] * (past - c * (kb // heads)).astype(F32))

    @pl.when(c == nc - 1)
    def _():
        nn = kn_ref.shape[0]
        attend(kn_ref, vn_ref, tabn_ref[...] - slope_ref[...] * token_offset(nn), jnp.zeros((1, rq), F32))
        o = (acc_ref[...] * (1.0 / l_ref[...])).T
        o_ref[...] = (o * _silu(g_ref[...])).astype(o_ref.dtype)


def _decode_count_tables(heads, past, kb, rq, n_new):
    per = kb // heads
    q_tok = jnp.arange(rq) // heads
    q_head = jnp.arange(rq) % heads

    def table(key_pos, key_head):
        dist = (past + q_tok)[None, :] - key_pos[:, None]
        cnt = _pattern_count(dist)
        ok = (key_head[:, None] == q_head[None, :]) & (cnt > 0)
        return jnp.where(ok, jnp.log(jnp.maximum(cnt, 1.0)) * LOG2E, NEG)

    slot = jnp.arange(kb)
    chunks = jnp.stack([table(c * per + slot // heads, slot % heads) for c in range(past * heads // kb)])
    new = jnp.arange(n_new)
    return chunks, table(past + new // heads, new % heads)


def _dil_decode(q, g, k_new, v_new, cache_k, cache_v, layer, slopes, heads, kb=2048):
    nb, rq, hd = q.shape
    past = cache_k.shape[2]
    nn = k_new.shape[1]
    assert rq % LANES == 0 and nn <= kb
    ck = cache_k.reshape(cache_k.shape[0], nb, past * heads, hd)
    cv = cache_v.reshape(cache_v.shape[0], nb, past * heads, hd)
    tab, tab_new = _decode_count_tables(heads, past, kb, rq, nn)
    small = pl.BlockSpec((None, rq, hd), lambda b, c: (b, 0, 0))
    cache = pl.BlockSpec((None, None, kb, hd), lambda b, c: (layer, b, c, 0))
    return pl.pallas_call(
        functools.partial(_dil_decode_kernel, heads=heads, past=past, kb=kb),
        out_shape=jax.ShapeDtypeStruct((nb, rq, hd), BF16),
        grid=(nb, past * heads // kb),
        in_specs=[pl.BlockSpec(memory_space=pltpu.SMEM), small, small, cache, cache, small, small,
                  pl.BlockSpec((None, kb, rq), lambda b, c: (c, 0, 0)),
                  pl.BlockSpec((nn, rq), lambda b, c: (0, 0))],
        out_specs=small,
        scratch_shapes=[pltpu.VMEM((hd, kb), BF16), pltpu.VMEM((kb, rq), F32), pltpu.VMEM((1, rq), F32),
                        pltpu.VMEM((1, rq), F32), pltpu.VMEM((1, rq), F32), pltpu.VMEM((hd, rq), F32)],
        compiler_params=_params(("arbitrary", "arbitrary"), VMEM_MID),
        name="dilated_decode",
    )(slopes, q, g, ck, cv, k_new, v_new, tab, tab_new)


def _diff_prompt_kernel(sc_ref, q_ref, k_ref, v_ref, g_ref, cn_ref, o_ref, kb_ref, vt_ref, bias_ref,
                        m1_ref, l1_ref, a1_ref, m2_ref, l2_ref, a2_ref, *, tile, heads, lam_init):
    h = pl.program_id(1)
    i = pl.program_id(2)
    slope = sc_ref[h]
    lam = sc_ref[heads]
    dk = C_QK_DIM

    @pl.when(i == 0)
    def _():
        _stage_kv(k_ref, v_ref, kb_ref, vt_ref)
        rel_t = (lax.broadcasted_iota(jnp.int32, (tile, tile), 1)
                 - lax.broadcasted_iota(jnp.int32, (tile, tile), 0))
        below = (-slope * LOG2E) * rel_t.astype(F32)
        bias_ref[0] = jnp.where(rel_t >= 0, below, NEG)
        bias_ref[1] = below

    stats = ((m1_ref, l1_ref, a1_ref), (m2_ref, l2_ref, a2_ref))
    _flash_t_reset(stats)
    qs = (q_ref[:, :dk].astype(BF16), q_ref[:, dk:].astype(BF16))

    def tile_step(j, bias_t):
        rows = pl.ds(pl.multiple_of(j * tile, tile), tile)
        cj = (slope * LOG2E) * ((i - j) * tile).astype(F32)
        _flash_t_tile((kb_ref[rows, :dk], kb_ref[rows, dk:]), vt_ref[:, rows], qs, bias_t, cj,
                      C_QK_DIM ** -0.5, stats)

    def body(j, carry):
        tile_step(j, bias_ref[1])
        return carry

    lax.fori_loop(0, i, body, 0)
    _flash_t_diag(lambda rows: (kb_ref[rows, :dk], kb_ref[rows, dk:]), vt_ref, pl.multiple_of(i * tile, tile),
                  lambda lo: tuple(qm[lo:, :] for qm in qs), bias_ref.at[0], C_QK_DIM ** -0.5, stats)
    o_t = a1_ref[...] * (1.0 / l1_ref[...]) - lam * (a2_ref[...] * (1.0 / l2_ref[...]))
    o_t = o_t * lax.rsqrt(jnp.mean(o_t * o_t, axis=0, keepdims=True) + EPS)
    cn = cn_ref[...]
    o_t = o_t * jnp.concatenate([cn] * (tile // LANES), axis=1) * (1.0 - lam_init)
    o_ref[...] = (o_t.T * _silu(g_ref[...])).astype(o_ref.dtype)


def _diff_prompt(q, k_stack, v_stack, g, layer, scalars, c_norm, nb, t, heads, lam_init, tile=512):
    dv = C_V_DIM
    nq = t // tile
    cn = jnp.broadcast_to(c_norm[:, :, None], (c_norm.shape[0], dv, LANES))
    return pl.pallas_call(
        functools.partial(_diff_prompt_kernel, tile=tile, heads=heads, lam_init=lam_init),
        out_shape=jax.ShapeDtypeStruct((nb * t, heads * dv), BF16),
        grid=(nb, heads, nq),
        in_specs=[pl.BlockSpec(memory_space=pltpu.SMEM),
                  pl.BlockSpec((tile, 2 * C_QK_DIM), lambda b, h, i: (b * nq + i, h)),
                  pl.BlockSpec((None, t, 2 * C_QK_DIM), lambda b, h, i: (layer, b, h)),
                  pl.BlockSpec((None, t, dv), lambda b, h, i: (layer, b, h)),
                  pl.BlockSpec((tile, dv), lambda b, h, i: (b * nq + i, h)),
                  pl.BlockSpec((None, dv, LANES), lambda b, h, i: (layer, 0, 0))],
        out_specs=pl.BlockSpec((tile, dv), lambda b, h, i: (b * nq + i, h)),
        scratch_shapes=[pltpu.VMEM((t, 2 * C_QK_DIM), BF16), pltpu.VMEM((dv, t), BF16),
                        pltpu.VMEM((2, tile, tile), F32),
                        pltpu.VMEM((1, tile), F32), pltpu.VMEM((1, tile), F32), pltpu.VMEM((dv, tile), F32),
                        pltpu.VMEM((1, tile), F32), pltpu.VMEM((1, tile), F32), pltpu.VMEM((dv, tile), F32)],
        compiler_params=_params(("arbitrary", "arbitrary", "arbitrary"), VMEM_BIG),
        name="diff_prompt",
    )(scalars, q, k_stack, v_stack, g, cn)


def _diff_decode_kernel(pt_ref, sc_ref, q_ref, g_ref, cn_ref, kp_ref, vp_ref, kp2_ref, vp2_ref, kn_ref, vn_ref,
                        o_ref, vt_ref, vt2_ref, bias_ref, slope_ref, m1_ref, l1_ref, a1_ref, m2_ref, l2_ref, a2_ref,
                        *, heads, page, n_pages, lam_init):
    step = pl.program_id(1)
    p_idx = 2 * step
    past = n_pages * page
    dk = C_QK_DIM
    rq = q_ref.shape[0]
    lam = sc_ref[heads]
    stats = ((m1_ref, l1_ref, a1_ref), (m2_ref, l2_ref, a2_ref))

    def head_and_offset(nk):
        key = lax.broadcasted_iota(jnp.int32, (nk, rq), 0)
        qry = lax.broadcasted_iota(jnp.int32, (nk, rq), 1)
        same_head = (key & (heads - 1)) == (qry & (heads - 1))
        return same_head, _div_pow2(qry, heads) - _div_pow2(key, heads)

    @pl.when(step == 0)
    def _():
        _flash_t_reset(stats)
        qh = lax.broadcasted_iota(jnp.int32, (1, rq), 1) & (heads - 1)
        slope = jnp.zeros((1, rq), F32)
        for hh in range(heads):
            slope = jnp.where(qh == hh, sc_ref[hh], slope)
        slope_ref[...] = slope * LOG2E
        same_head, off = head_and_offset(page * heads)
        bias_ref[...] = jnp.where(same_head, -(slope * LOG2E) * off.astype(F32), NEG)

    qs = (q_ref[:, :dk].astype(BF16), q_ref[:, dk:].astype(BF16))

    def attend(k_ref, v_ref, t_ref, bias_t, shift):
        nk = k_ref.shape[0]
        for r0 in range(0, nk, LANES):
            t_ref[:, r0:r0 + LANES] = v_ref[r0:r0 + LANES, :].T.astype(BF16)
        kb = k_ref[...].astype(BF16)
        _flash_t_tile((kb[:, :dk], kb[:, dk:]), t_ref[:, 0:nk], qs, bias_t, shift, C_QK_DIM ** -0.5, stats)

    attend(kp_ref, vp_ref, vt_ref, bias_ref[...], slope_ref[...] * (past - p_idx * page).astype(F32))
    attend(kp2_ref, vp2_ref, vt2_ref, bias_ref[...], slope_ref[...] * (past - (p_idx + 1) * page).astype(F32))

    @pl.when(step == n_pages // 2 - 1)
    def _():
        same_head, off = head_and_offset(kn_ref.shape[0])
        bias_new = jnp.where(same_head & (off >= 0), -slope_ref[...] * off.astype(F32), NEG)
        attend(kn_ref, vn_ref, vt_ref, bias_new, jnp.zeros((1, rq), F32))
        o_t = a1_ref[...] * (1.0 / l1_ref[...]) - lam * (a2_ref[...] * (1.0 / l2_ref[...]))
        o_t = o_t * lax.rsqrt(jnp.mean(o_t * o_t, axis=0, keepdims=True) + EPS)
        o_t = o_t * jnp.concatenate([cn_ref[...]] * (rq // LANES), axis=1) * (1.0 - lam_init)
        o_ref[...] = (o_t.T * _silu(g_ref[...])).astype(o_ref.dtype)


def _diff_decode(q, g, k_new, v_new, cache_k, cache_v, page_table, layer, scalars, c_norm, heads, lam_init):
    nb, rq, _ = q.shape
    dv = C_V_DIM
    n_odd, n_pool, page = cache_k.shape[:3]
    n_pages = page_table.shape[1]
    nk = page * heads
    assert rq % LANES == 0 and k_new.shape[1] <= nk and n_pages % 2 == 0
    ck = cache_k.reshape(n_odd, n_pool, nk, 2 * C_QK_DIM)
    cv = cache_v.reshape(n_odd, n_pool, nk, dv)
    cn = jnp.broadcast_to(c_norm[:, :, None], (c_norm.shape[0], dv, LANES))
    small_k = pl.BlockSpec((None, rq, 2 * C_QK_DIM), lambda b, p, pt: (b, 0, 0))
    small_v = pl.BlockSpec((None, rq, dv), lambda b, p, pt: (b, 0, 0))
    stat = [pltpu.VMEM((1, rq), F32), pltpu.VMEM((1, rq), F32), pltpu.VMEM((dv, rq), F32)]
    grid_spec = pltpu.PrefetchScalarGridSpec(
        num_scalar_prefetch=1,
        grid=(nb, n_pages // 2),
        in_specs=[pl.BlockSpec(memory_space=pltpu.SMEM), small_k, small_v,
                  pl.BlockSpec((None, dv, LANES), lambda b, p, pt: (layer, 0, 0)),
                  pl.BlockSpec((None, None, nk, 2 * C_QK_DIM), lambda b, p, pt: (layer, pt[b, 2 * p], 0, 0)),
                  pl.BlockSpec((None, None, nk, dv), lambda b, p, pt: (layer, pt[b, 2 * p], 0, 0)),
                  pl.BlockSpec((None, None, nk, 2 * C_QK_DIM), lambda b, p, pt: (layer, pt[b, 2 * p + 1], 0, 0)),
                  pl.BlockSpec((None, None, nk, dv), lambda b, p, pt: (layer, pt[b, 2 * p + 1], 0, 0)),
                  small_k, small_v],
        out_specs=small_v,
        scratch_shapes=[pltpu.VMEM((dv, nk), BF16), pltpu.VMEM((dv, nk), BF16),
                        pltpu.VMEM((nk, rq), F32), pltpu.VMEM((1, rq), F32)] + stat + stat)
    return pl.pallas_call(
        functools.partial(_diff_decode_kernel, heads=heads, page=page, n_pages=n_pages, lam_init=lam_init),
        out_shape=jax.ShapeDtypeStruct((nb, rq, dv), BF16),
        grid_spec=grid_spec,
        compiler_params=_params(("arbitrary", "arbitrary"), VMEM_MID),
        name="diff_decode",
    )(page_table, scalars, q, g, cn, ck, cv, ck, cv, k_new, v_new)


def _expand_exact(x, onehot3):
    hi = x.astype(BF16)
    r1 = x - hi.astype(F32)
    mid = r1.astype(BF16)
    lo = (r1 - mid.astype(F32)).astype(BF16)
    return jnp.dot(jnp.concatenate([hi, mid, lo], axis=1), onehot3, preferred_element_type=F32)


def _ssd_kernel(xs_ref, bm_ref, cm_ref, z_ref, dt_ref, wx_ref, wb_ref, wc_ref, bx_ref, bb_ref, bc_ref,
                dtb_ref, alog_ref, dskip_ref, bnorm_ref, *rest, rows, t_valid, n_heads, has_past):
    if has_past:
        (cx_ref, cbm_ref, ccm_ref, h0_ref, y_ref, hout_ref,
         px_ref, pb_ref, pc_ref, h_ref, at_ref, ohc_ref, ohb_ref) = rest
    else:
        y_ref, hout_ref, px_ref, pb_ref, pc_ref, h_ref, at_ref, ohc_ref, ohb_ref = rest
    g = pl.program_id(1)
    c = pl.program_id(2)
    ln = B_CHUNK
    hpg = n_heads // B_GROUPS
    gw = hpg * B_HEAD_DIM
    pad = 8

    @pl.when(c == 0)
    def _():
        if has_past:
            px_ref[0:pad, :] = cx_ref[...]
            pb_ref[0:pad, :] = cbm_ref[...]
            pc_ref[0:pad, :] = ccm_ref[...]
            h_ref[...] = h0_ref[...]
        else:
            px_ref[0:pad, :] = jnp.zeros((pad, gw), F32)
            pb_ref[0:pad, :] = jnp.zeros((pad, B_STATE), F32)
            pc_ref[0:pad, :] = jnp.zeros((pad, B_STATE), F32)
            h_ref[...] = jnp.zeros(h_ref.shape, F32)
        if rows < ln:
            for p_ref in (px_ref, pb_ref, pc_ref):
                p_ref[pad + rows:pad + ln, :] = jnp.zeros((ln - rows, p_ref.shape[1]), F32)
        head_of_lane = g * hpg + _div_pow2(lax.broadcasted_iota(jnp.int32, (LANES, gw), 1), B_HEAD_DIM)
        oh_ch = jnp.where(lax.broadcasted_iota(jnp.int32, (LANES, gw), 0) == head_of_lane, 1.0, 0.0).astype(BF16)
        head_of_blk = g * hpg + _div_pow2(lax.broadcasted_iota(jnp.int32, (LANES, hpg * ln), 1), ln)
        oh_blk = jnp.where(lax.broadcasted_iota(jnp.int32, (LANES, hpg * ln), 0) == head_of_blk,
                           1.0, 0.0).astype(BF16)
        for part in range(3):
            ohc_ref[part * LANES:(part + 1) * LANES, :] = oh_ch
            ohb_ref[part * LANES:(part + 1) * LANES, :] = oh_blk

    def conv(raw_ref, p_ref, w_ref, b_ref):
        p_ref[pad:pad + rows, :] = raw_ref[...]
        out = b_ref[...] + p_ref[pad - 3:pad - 3 + ln, :] * w_ref[0:1, :]
        for j in range(1, B_CONV):
            out = out + p_ref[pad - 3 + j:pad - 3 + j + ln, :] * w_ref[j:j + 1, :]
        if rows == ln:
            p_ref[0:pad, :] = p_ref[ln:ln + pad, :]
        return _silu(out)

    xs = conv(xs_ref, px_ref, wx_ref, bx_ref)
    bm = conv(bm_ref, pb_ref, wb_ref, bb_ref)
    cm = conv(cm_ref, pc_ref, wc_ref, bc_ref)

    lane = lax.broadcasted_iota(jnp.int32, (ln, LANES), 1)
    rowi = lax.broadcasted_iota(jnp.int32, (ln, LANES), 0)
    if rows < ln:
        dt_raw = jnp.concatenate([dt_ref[...], jnp.zeros((ln - rows, LANES), F32)], axis=0)
    else:
        dt_raw = dt_ref[...]
    live = (lane < n_heads) & (rowi < t_valid)
    dt = jnp.where(live, jax.nn.softplus(jnp.where(live, dt_raw, 0.0) + dtb_ref[...]), 0.0)
    a_step = dt * (-jnp.exp(alog_ref[...]))
    tri = jnp.where(lax.broadcasted_iota(jnp.int32, (ln, ln), 0) >= lax.broadcasted_iota(jnp.int32, (ln, ln), 1),
                    1.0, 0.0)
    acum = jnp.dot(tri, a_step, preferred_element_type=F32, precision=lax.Precision.HIGHEST)
    at_ref[...] = acum.T

    dt_e = _expand_exact(dt, ohc_ref[...])
    ac_e = _expand_exact(acum, ohc_ref[...])
    ac_col = _expand_exact(acum, ohb_ref[...])
    last_e = ac_e[ln - 1:ln, :]

    xdt = xs * dt_e
    xdt_b = xdt.astype(BF16)
    w_end = (xdt * jnp.exp(last_e - ac_e)).astype(BF16)
    bm_b = bm.astype(BF16)
    cm_b = cm.astype(BF16)
    cb = lax.dot_general(cm_b, bm_b, (((1,), (1,)), ((), ())), preferred_element_type=F32)
    h_prev = h_ref[...]
    y = jnp.dot(cm_b, h_prev.astype(BF16), preferred_element_type=F32) * jnp.exp(ac_e)

    causal = lax.broadcasted_iota(jnp.int32, (ln, ln), 0) >= lax.broadcasted_iota(jnp.int32, (ln, ln), 1)
    lane_pair = lax.broadcasted_iota(jnp.int32, (ln, 2 * B_HEAD_DIM), 1)
    y_parts = []
    for pr in range(hpg // 2):
        ms = []
        for r in (2 * pr, 2 * pr + 1):
            seg = ac_col[:, r * ln:(r + 1) * ln] - at_ref[pl.ds(g * hpg + r, 1), :]
            decay = jnp.exp(jnp.where(causal, seg, -jnp.inf))
            ms.append((cb * decay).astype(BF16))
        xp = xdt_b[:, pr * 2 * B_HEAD_DIM:(pr + 1) * 2 * B_HEAD_DIM]
        zero = jnp.zeros_like(xp)
        rhs = jnp.concatenate([jnp.where(lane_pair < B_HEAD_DIM, xp, zero),
                               jnp.where(lane_pair >= B_HEAD_DIM, xp, zero)], axis=0)
        y_parts.append(jnp.dot(jnp.concatenate(ms, axis=1), rhs, preferred_element_type=F32))
    y = y + jnp.concatenate(y_parts, axis=1)

    upd = jnp.dot(bm.T.astype(BF16), w_end, preferred_element_type=F32)
    h_new = h_prev * jnp.exp(last_e) + upd
    h_ref[...] = h_new
    hout_ref[...] = h_new

    y = (y + dskip_ref[...] * xs)
    if rows < ln:
        y = y[0:rows, :]
    y = y * _silu(z_ref[...])
    y = y * lax.rsqrt(jnp.mean(y * y, axis=-1, keepdims=True) + EPS) * bnorm_ref[...]
    y_ref[...] = y.astype(y_ref.dtype)


def _ssd(proj, nb, t_rows, t_valid, xc, bc, cc, zc, dtc, layer, conv_w, conv_b, dt_bias_p, a_log_p,
         d_skip_e, b_norm, past=None):
    ln = B_CHUNK
    rows = min(ln, t_rows)
    nc = t_rows // rows
    width = d_skip_e.shape[1]
    n_heads = width // B_HEAD_DIM
    gw = width // B_GROUPS
    st = B_STATE
    xoff = 0
    boff = width // st
    coff = boff + B_GROUPS
    ne = conv_w.shape[0]
    cbias = conv_b.reshape(ne, 1, conv_b.shape[1])
    row_blk = lambda b, g, c: b * nc + c
    in_specs = [
        pl.BlockSpec((rows, gw), lambda b, g, c: (row_blk(b, g, c), xc + g)),
        pl.BlockSpec((rows, st), lambda b, g, c: (row_blk(b, g, c), bc + g)),
        pl.BlockSpec((rows, st), lambda b, g, c: (row_blk(b, g, c), cc + g)),
        pl.BlockSpec((rows, gw), lambda b, g, c: (row_blk(b, g, c), zc + g)),
        pl.BlockSpec((rows, LANES), lambda b, g, c: (row_blk(b, g, c), dtc)),
        pl.BlockSpec((None, B_CONV, gw), lambda b, g, c: (layer, 0, xoff + g)),
        pl.BlockSpec((None, B_CONV, st), lambda b, g, c: (layer, 0, boff + g)),
        pl.BlockSpec((None, B_CONV, st), lambda b, g, c: (layer, 0, coff + g)),
        pl.BlockSpec((None, 1, gw), lambda b, g, c: (layer, 0, xoff + g)),
        pl.BlockSpec((None, 1, st), lambda b, g, c: (layer, 0, boff + g)),
        pl.BlockSpec((None, 1, st), lambda b, g, c: (layer, 0, coff + g)),
        pl.BlockSpec((None, 1, LANES), lambda b, g, c: (layer, 0, 0)),
        pl.BlockSpec((None, 1, LANES), lambda b, g, c: (layer, 0, 0)),
        pl.BlockSpec((None, 1, gw), lambda b, g, c: (layer, 0, g)),
        pl.BlockSpec((None, 1, gw), lambda b, g, c: (layer, 0, g)),
    ]
    args = [proj, proj, proj, proj, proj, conv_w, conv_w, conv_w, cbias, cbias, cbias,
            dt_bias_p, a_log_p, d_skip_e.reshape(ne, 1, width), b_norm.reshape(ne, 1, width)]
    if past is not None:
        conv_hist, h0t = past
        in_specs += [
            pl.BlockSpec((None, 8, gw), lambda b, g, c: (b, 0, xoff + g)),
            pl.BlockSpec((None, 8, st), lambda b, g, c: (b, 0, boff + g)),
            pl.BlockSpec((None, 8, st), lambda b, g, c: (b, 0, coff + g)),
            pl.BlockSpec((None, st, gw), lambda b, g, c: (b, 0, g)),
        ]
        args += [conv_hist, conv_hist, conv_hist, h0t]
    y, h_t = pl.pallas_call(
        functools.partial(_ssd_kernel, rows=rows, t_valid=t_valid, n_heads=n_heads, has_past=past is not None),
        out_shape=[jax.ShapeDtypeStruct((nb * t_rows, width), BF16),
                   jax.ShapeDtypeStruct((nb, st, width), F32)],
        grid=(nb, B_GROUPS, nc),
        in_specs=in_specs,
        out_specs=[pl.BlockSpec((rows, gw), lambda b, g, c: (row_blk(b, g, c), g)),
                   pl.BlockSpec((None, st, gw), lambda b, g, c: (b, 0, g))],
        scratch_shapes=[pltpu.VMEM((ln + 8, gw), F32), pltpu.VMEM((ln + 8, st), F32),
                        pltpu.VMEM((ln + 8, st), F32), pltpu.VMEM((st, gw), F32),
                        pltpu.VMEM((LANES, ln), F32),
                        pltpu.VMEM((3 * LANES, gw), BF16), pltpu.VMEM((3 * LANES, n_heads // B_GROUPS * ln), BF16)],
        compiler_params=_params(("arbitrary", "arbitrary", "arbitrary"), VMEM_MID),
        name="ssd",
    )(*args)
    return y, h_t


def _shift_kernel(a_ref, nxt_ref, new_ref, o_ref, *, blk, n_blocks):
    g = pl.program_id(1)
    o_ref[0:blk - 1] = a_ref[1:blk]

    @pl.when(g < n_blocks - 1)
    def _():
        o_ref[blk - 1] = nxt_ref[0]

    @pl.when(g == n_blocks - 1)
    def _():
        o_ref[blk - 1] = new_ref[0]


def _shift_window(cache, new_rows, blk=64):
    e, b, past, heads, hd = cache.shape
    dec = new_rows.shape[2]
    assert past % (dec * blk) == 0
    grp = dec * heads
    n_grp = past // dec
    n_blocks = n_grp // blk
    c4 = cache.reshape(e * b, n_grp, grp, hd)
    n4 = new_rows.reshape(e * b, 1, grp, hd)
    out = pl.pallas_call(
        functools.partial(_shift_kernel, blk=blk, n_blocks=n_blocks),
        out_shape=jax.ShapeDtypeStruct(c4.shape, cache.dtype),
        grid=(e * b, n_blocks),
        in_specs=[pl.BlockSpec((None, blk, grp, hd), lambda n, g: (n, g, 0, 0)),
                  pl.BlockSpec((None, 1, grp, hd), lambda n, g: (n, jnp.minimum((g + 1) * blk, n_grp - 1), 0, 0)),
                  pl.BlockSpec((None, 1, grp, hd), lambda n, g: (n, 0, 0, 0))],
        out_specs=pl.BlockSpec((None, blk, grp, hd), lambda n, g: (n, g, 0, 0)),
        compiler_params=_params(("arbitrary", "arbitrary"), VMEM_MID),
        name="window_shift",
    )(c4, c4, n4)
    return out.reshape(cache.shape)


def _alibi_slopes(n_heads):
    return jnp.exp2(-8.0 * jnp.arange(1, n_heads + 1, dtype=F32) / n_heads)


def _pad_lanes(v):
    return jnp.pad(v, ((0, 0), (0, LANES - v.shape[1]))).reshape(v.shape[0], 1, LANES)


def kernel(x_prompt, x_sample, cache_win_k, cache_win_v, state_conv, state_ssm, cache_diff_k, cache_diff_v,
           page_table, c_prompt, c_sample, ln_g, ln_b, w_mod, b_mod, w_in_even, w_out_even, conv_w, conv_b,
           dt_bias, a_log, d_skip, b_norm, w_in_odd, w_out_odd, lam_q1, lam_k1, lam_q2, lam_k2, c_norm):
    nbp, seq, d = x_prompt.shape
    nbs, dec_seq, _ = x_sample.shape
    depth = w_mod.shape[0]
    alpha = (2 * depth) ** 0.25
    a_heads = cache_win_k.shape[3]
    a_width = a_heads * A_HEAD_DIM
    c_heads = cache_diff_k.shape[3]
    b_width = b_norm.shape[1]
    conv_dim = conv_w.shape[2]
    b_heads = dt_bias.shape[1]
    assert dec_seq <= DEC_PAD and seq % B_CHUNK == 0

    slopes_a = _alibi_slopes(a_heads)
    slopes_c = _alibi_slopes(c_heads)

    c_rows = jnp.concatenate([c_prompt, c_sample], axis=0)
    c_rows = jnp.pad(c_rows, ((0, (-c_rows.shape[0]) % 8), (0, 0)))
    mods = _mods(c_rows, w_mod, b_mod)
    mods_p = mods[:, :nbp].reshape(depth, nbp, 1, 3 * d)
    mods_s = jnp.repeat(mods[:, nbp:nbp + nbs], DEC_PAD, axis=1).reshape(depth, 1, nbs * DEC_PAD, 3 * d)

    xp = x_prompt.reshape(nbp * seq, d)
    xs = jnp.pad(x_sample, ((0, 0), (0, DEC_PAD - dec_seq), (0, 0))).reshape(nbs * DEC_PAD, d)
    ms_rows = nbs * DEC_PAD
    tm_row = 256

    hp = _modulate(xp, mods_p, 0, tm_row, seq)
    hs = _modulate(xs, mods_s, 0, ms_rows, ms_rows)

    dt_bias_p, a_log_p = _pad_lanes(dt_bias), _pad_lanes(a_log)
    d_skip_e = jnp.repeat(d_skip, B_HEAD_DIM, axis=1)
    conv_hist = jnp.pad(state_conv, ((0, 0), (0, 0), (8 - (B_CONV - 1), 0), (0, 0)))
    h0t = state_ssm.transpose(0, 1, 4, 2, 3).reshape(state_ssm.shape[0], nbs, B_STATE, b_width)

    def rows_th(a, width):
        return a.reshape(nbs, DEC_PAD * (a.shape[1] // width), width)

    n_even, n_odd = w_in_even.shape[0], w_in_odd.shape[0]
    past_win = cache_win_k.shape[2]
    assert seq <= A_PATTERNS[-1][0] and past_win == A_PATTERNS[-1][0]
    w_in_even_t = jnp.swapaxes(w_in_even, 1, 2)
    wk_p = wv_p = dk_p = dv_p = None
    outs = {k: [] for k in ("cv_p", "ss_p", "wk_s", "wv_s", "cv_s", "ss_s", "dk_s", "dv_s")}
    for l in range(depth):
        last = l == depth - 1
        if l % 2 == 0:
            e = l // 2
            gw = b_width // B_GROUPS
            r0 = 3 * a_width
            z0, x0 = a_width, a_width + b_width
            ssd_cols = dict(xc=x0 // gw, bc=(x0 + b_width) // LANES, cc=(x0 + b_width) // LANES + B_GROUPS,
                            zc=z0 // gw, dtc=(x0 + conv_dim) // LANES)
            ssd_w = dict(layer=e, conv_w=conv_w, conv_b=conv_b, dt_bias_p=dt_bias_p, a_log_p=a_log_p,
                         d_skip_e=d_skip_e, b_norm=b_norm)
            proj = functools.partial(_mm, [hp], w_in_even_t, e, 1024, extra=[hs], w_t=True)
            aq_p, aq_s = proj(0, a_width)
            wk_p, ak_s = proj(a_width, a_width, stack=(wk_p, n_even))
            wv_p, av_s = proj(2 * a_width, a_width, stack=(wv_p, n_even))
            rest_p, rest_s = proj(r0)
            ya_p = _dil_prompt(aq_p, wk_p, wv_p, rest_p, e, slopes_a, nbp, seq, a_heads, 0)
            yb_p, ht_p = _ssd(rest_p, nbp, seq, seq, **ssd_cols, **ssd_w)
            outs["cv_p"].append(rest_p.reshape(nbp, seq, -1)[:, seq - (B_CONV - 1):, x0:x0 + conv_dim])
            outs["ss_p"].append(ht_p.reshape(nbp, B_STATE, b_heads, B_HEAD_DIM).transpose(0, 2, 3, 1))
            ya_s = _dil_decode(rows_th(aq_s, A_HEAD_DIM), rows_th(rest_s[:, :a_width], A_HEAD_DIM),
                               rows_th(ak_s, A_HEAD_DIM), rows_th(av_s, A_HEAD_DIM),
                               cache_win_k, cache_win_v, e, slopes_a, a_heads)
            ya_s = ya_s.reshape(ms_rows, a_width)
            yb_s, ht_s = _ssd(rest_s, nbs, DEC_PAD, dec_seq, past=(conv_hist[e], h0t[e]), **ssd_cols, **ssd_w)
            new_rows = lambda a: a.reshape(nbs, DEC_PAD, a_heads, A_HEAD_DIM)[:, :dec_seq]
            outs["wk_s"].append(new_rows(ak_s))
            outs["wv_s"].append(new_rows(av_s))
            xbc_new = rest_s.reshape(nbs, DEC_PAD, -1)[:, :dec_seq, x0:x0 + conv_dim]
            outs["cv_s"].append(jnp.concatenate([state_conv[e], xbc_new], axis=1)[:, -(B_CONV - 1):])
            outs["ss_s"].append(ht_s.reshape(nbs, B_STATE, b_heads, B_HEAD_DIM).transpose(0, 2, 3, 1))
            yp, ys = _mm([ya_p, yb_p], w_out_even, e, 512, extra=[ya_s, yb_s])
        else:
            o = l // 2
            lam_init = 0.8 - 0.6 * math.exp(-0.3 * l)
            lam = (jnp.exp(jnp.sum(lam_q1[o] * lam_k1[o])) - jnp.exp(jnp.sum(lam_q2[o] * lam_k2[o])) + lam_init)
            scalars = jnp.concatenate([slopes_c, lam.reshape(1)])
            qkw = c_heads * 2 * C_QK_DIM
            vw = c_heads * C_V_DIM
            proj = functools.partial(_mm, [hp], w_in_odd, o, 1024, extra=[hs])
            q_p, q_s = proj(0, qkw)
            dk_p, k_s = proj(qkw, qkw, stack=(dk_p, n_odd))
            dv_p, v_s = proj(2 * qkw, vw, stack=(dv_p, n_odd))
            g_p, g_s = proj(2 * qkw + vw, vw)
            y_p = _diff_prompt(q_p, dk_p, dv_p, g_p, o, scalars, c_norm, nbp, seq, c_heads, lam_init)
            y_s = _diff_decode(rows_th(q_s, 2 * C_QK_DIM), rows_th(g_s, C_V_DIM), rows_th(k_s, 2 * C_QK_DIM),
                               rows_th(v_s, C_V_DIM), cache_diff_k, cache_diff_v, page_table, o, scalars, c_norm,
                               c_heads, lam_init)
            outs["dk_s"].append(k_s.reshape(nbs, DEC_PAD, c_heads, 2 * C_QK_DIM)[:, :dec_seq])
            outs["dv_s"].append(v_s.reshape(nbs, DEC_PAD, c_heads, C_V_DIM)[:, :dec_seq])
            yp, ys = _mm([y_p], w_out_odd, o, 1024, extra=[y_s.reshape(ms_rows, vw)])
        xp, hp = _post(xp, yp, mods_p, l, ln_g, ln_b, alpha, tm_row, seq, not last)
        xs, hs = _post(xs, ys, mods_s, l, ln_g, ln_b, alpha, ms_rows, ms_rows, not last)

    y_prompt = xp.reshape(nbp, seq, d)
    y_sample = xs.reshape(nbs, DEC_PAD, d)[:, :dec_seq]
    st = lambda k: jnp.stack(outs[k])
    wk_s = _shift_window(cache_win_k, st("wk_s"))
    wv_s = _shift_window(cache_win_v, st("wv_s"))
    return (y_prompt, y_sample,
            wk_p.reshape(n_even, nbp, seq, a_heads, A_HEAD_DIM), wv_p.reshape(n_even, nbp, seq, a_heads, A_HEAD_DIM),
            st("cv_p"), st("ss_p"),
            dk_p.reshape(n_odd, nbp, seq, c_heads, 2 * C_QK_DIM), dv_p.reshape(n_odd, nbp, seq, c_heads, C_V_DIM),
            wk_s, wv_s, st("cv_s"), st("ss_s"), st("dk_s"), st("dv_s"))
```
